```python
import math
import jax, jax.numpy as jnp
from jax import lax
import numpy as np

D_MODEL = 1024
BATCH = 2
SEQ = 8192
DEPTH = 1
DEC_BATCH = 128
DEC_SEQ = 4
PAST_LEN = 2048
PAGE_SIZE = 128

ATT_HD = 64
ATT_HEADS = D_MODEL // (2 * ATT_HD)
ATT_WIDTH = ATT_HEADS * 2 * ATT_HD
ROT_DIM = ATT_HD // 4
ROPE_THETA = 500000.0
Q_BLOCK = 128
SSD_HD = 64
SSD_INNER = D_MODEL
SSD_HEADS = SSD_INNER // SSD_HD
SSD_GROUPS = 2
D_STATE = 128
CONV_W = 4
CONV_DIM = SSD_INNER + 2 * SSD_GROUPS * D_STATE
SSD_CHUNK = 128
DT_MIN = 0.001
DT_MAX = 0.1
D_MIX = ATT_WIDTH + SSD_INNER
IN_DIM = 3 * ATT_WIDTH + SSD_INNER + CONV_DIM + SSD_HEADS
SPLITS = [ATT_WIDTH, 2 * ATT_WIDTH, 3 * ATT_WIDTH, 3 * ATT_WIDTH + SSD_INNER,
          3 * ATT_WIDTH + SSD_INNER + CONV_DIM]
D_FF = ((8 * D_MODEL + 3 * 256 - 1) // (3 * 256)) * 256
P_DIM = 256
DEEPNORM_ALPHA = (2 * DEPTH) ** 0.25
DEEPNORM_BETA = (8 * DEPTH) ** -0.25
EPS = 1e-5

kernel_name = "hymba_diffattn_ssd_deepnorm_step"

F32 = jnp.float32


def layer_norm(x, g, b):
    xf = x.astype(F32)
    mu = jnp.mean(xf, -1, keepdims=True)
    var = jnp.mean(jnp.square(xf - mu), -1, keepdims=True)
    return ((xf - mu) * lax.rsqrt(var + EPS) * g.astype(F32) + b.astype(F32)).astype(x.dtype)


def rms_norm(x):
    xf = x.astype(F32)
    return xf * lax.rsqrt(jnp.mean(jnp.square(xf), -1, keepdims=True) + EPS)


def rope_partial(x, pos):
    half = ROT_DIM // 2
    inv = ROPE_THETA ** (-jnp.arange(half, dtype=F32) * 2.0 / ROT_DIM)
    ang = pos.astype(F32)[:, None] * inv[None, :]
    cos = jnp.cos(ang)[None, :, None, :]
    sin = jnp.sin(ang)[None, :, None, :]
    xf = x.astype(F32)
    x1 = xf[..., :half]
    x2 = xf[..., half:ROT_DIM]
    out = jnp.concatenate([x1 * cos - x2 * sin, x2 * cos + x1 * sin, xf[..., ROT_DIM:]], -1)
    return out.astype(x.dtype)


def diff_attention(q, q_pos, segments, lam):
    b, l = q.shape[:2]
    blk = Q_BLOCK if l % Q_BLOCK == 0 else l
    nb = l // blk
    scale = ATT_HD ** -0.5
    qb = jnp.swapaxes(q.reshape(b, nb, blk, 2 * ATT_HEADS, ATT_HD), 0, 1)
    pb = q_pos.reshape(nb, blk)
    segs = [(k.astype(F32), v.astype(F32), kp) for k, v, kp in segments]
    sizes = [s[0].shape[1] for s in segs]

    def block(args):
        qq, pp = args
        qf = qq.astype(F32) * scale
        scores = []
        for k, _, kp in segs:
            s = jnp.einsum('bqcd,bkcd->bcqk', qf, k)
            scores.append(jnp.where(kp[None, None, None, :] <= pp[None, None, :, None], s, -jnp.inf))
        s = jnp.concatenate(scores, -1) if len(scores) > 1 else scores[0]
        pr = jax.nn.softmax(s, axis=-1).reshape(b, ATT_HEADS, 2, blk, -1)
        a = pr[:, :, 0] - lam * pr[:, :, 1]
        outs = []
        off = 0
        for (_, v, _), n in zip(segs, sizes):
            outs.append(jnp.einsum('bhqk,bkhe->bqhe', a[..., off:off + n], v))
            off += n
        return sum(outs[1:], outs[0])

    o = lax.map(block, (qb, pb))
    return jnp.swapaxes(o, 0, 1).reshape(b, l, ATT_HEADS, 2 * ATT_HD)


def causal_conv(xbc, conv_state, w, bias):
    l = xbc.shape[1]
    xpad = jnp.concatenate([conv_state.astype(xbc.dtype), xbc], 1)
    y = sum(xpad[:, j:j + l].astype(F32) * w[j].astype(F32) for j in range(CONV_W)) + bias.astype(F32)
    return jax.nn.silu(y), xpad[:, -(CONV_W - 1):]


def ssd_chunked(x, dt, A, Bm, Cm, s0, chunk):
    b, l, h, p = x.shape
    g, n = Bm.shape[2:]
    r = h // g
    c = l // chunk
    X = (x.astype(F32) * dt[..., None]).reshape(b, c, chunk, g, r, p)
    dA = (dt * A).reshape(b, c, chunk, g, r).transpose(0, 3, 4, 1, 2)
    Bc = Bm.astype(F32).reshape(b, c, chunk, g, n)
    Cc = Cm.astype(F32).reshape(b, c, chunk, g, n)
    cs = jnp.cumsum(dA, -1)
    seg = cs[..., :, None] - cs[..., None, :]
    mask = jnp.tril(jnp.ones((chunk, chunk), bool))
    Lm = jnp.where(mask, jnp.exp(jnp.where(mask, seg, 0.0)), 0.0)
    CB = jnp.einsum('bclgn,bcsgn->bgcls', Cc, Bc)
    y_diag = jnp.einsum('bgrcls,bcsgrp->bclgrp', CB[:, :, None] * Lm, X)
    decay = jnp.exp(cs[..., -1:] - cs)
    states = jnp.einsum('bcqgn,bgrcq,bcqgrp->bcgrpn', Bc, decay, X)
    chunk_decay = jnp.exp(cs[..., -1])

    def step(carry, inp):
        st, dec = inp
        return carry * dec[..., None, None] + st, carry

    final, prev = lax.scan(step, s0.astype(F32).reshape(b, g, r, p, n),
                           (jnp.moveaxis(states, 1, 0), jnp.moveaxis(chunk_decay, -1, 0)))
    prev = jnp.moveaxis(prev, 0, 1)
    y_off = jnp.einsum('bclgn,bgrcl,bcgrpn->bclgrp', Cc, jnp.exp(cs), prev)
    return (y_diag + y_off).reshape(b, l, h, p), final.reshape(b, h, p, n)


def hybrid_layer(x, pe, pos, past, ssm0, conv0, prm, lam_init):
    (w_in, conv_w, conv_b, dt_bias, a_log, d_skip, ssd_norm_w, lq1, lk1, lq2, lk2, subln_w,
     w_out, ln1_g, ln1_b, w_ffn_in, w_ffn_out, w_pe, w_pg, ln2_g, ln2_b) = prm
    b, l, _ = x.shape
    proj = x @ w_in
    q, k, v, z, xbc, dt_raw = jnp.split(proj, SPLITS, axis=-1)
    q = rope_partial(q.reshape(b, l, 2 * ATT_HEADS, ATT_HD), pos)
    k = rope_partial(k.reshape(b, l, 2 * ATT_HEADS, ATT_HD), pos)
    v = v.reshape(b, l, ATT_HEADS, 2 * ATT_HD)
    lam = (jnp.exp(jnp.sum(lq1.astype(F32) * lk1.astype(F32)))
           - jnp.exp(jnp.sum(lq2.astype(F32) * lk2.astype(F32))) + lam_init)
    if past is None:
        segments = [(k, v, pos)]
    else:
        segments = [(past[0], past[1], jnp.arange(past[0].shape[1])), (k, v, pos)]
    att = diff_attention(q, pos, segments, lam)
    att = (rms_norm(att) * subln_w.astype(F32) * (1.0 - lam_init)).reshape(b, l, ATT_WIDTH)
    xbc_act, conv_new = causal_conv(xbc, conv0, conv_w, conv_b)
    xs, Bm, Cm = jnp.split(xbc_act, [SSD_INNER, SSD_INNER + SSD_GROUPS * D_STATE], axis=-1)
    dt = jax.nn.softplus(dt_raw.astype(F32) + dt_bias.astype(F32))
    A = -jnp.exp(a_log.astype(F32))
    xs = xs.reshape(b, l, SSD_HEADS, SSD_HD)
    chunk = SSD_CHUNK if l % SSD_CHUNK == 0 else l
    y, ssm_new = ssd_chunked(xs, dt, A, Bm.reshape(b, l, SSD_GROUPS, D_STATE),
                             Cm.reshape(b, l, SSD_GROUPS, D_STATE), ssm0, chunk)
    y = y + d_skip.astype(F32)[:, None] * xs
    gz = (y.reshape(b, l, SSD_INNER) * jax.nn.silu(z.astype(F32))).reshape(b, l, SSD_GROUPS, SSD_INNER // SSD_GROUPS)
    ssd_out = rms_norm(gz).reshape(b, l, SSD_INNER) * ssd_norm_w.astype(F32)
    mix = jnp.concatenate([att, ssd_out], -1).astype(x.dtype) @ w_out
    h = layer_norm(DEEPNORM_ALPHA * x + mix, ln1_g, ln1_b)
    gate, up = jnp.split(h @ w_ffn_in, 2, axis=-1)
    ffn = (jax.nn.silu(gate) * up) @ w_ffn_out
    pemb = (pe @ w_pe) * jax.nn.sigmoid(h @ w_pg)
    out = layer_norm(DEEPNORM_ALPHA * h + ffn + pemb, ln2_g, ln2_b)
    return out, k, v, ssm_new, conv_new


def setup_inputs(seed: int = 0) -> dict:
    key = jax.random.key(seed)
    ks = jax.random.split(key, 32)
    n_pages = PAST_LEN // PAGE_SIZE
    n_used = DEC_BATCH * n_pages
    n_pool = (n_used * 5 + 3) // 4

    def nrm(k, shape, s):
        return jax.random.normal(k, shape, F32) * s

    x_prompt = nrm(ks[0], (BATCH, SEQ, D_MODEL), 1.0)
    x_sample = nrm(ks[1], (DEC_BATCH, DEC_SEQ, D_MODEL), 1.0)
    p_prompt = nrm(ks[2], (DEPTH, BATCH, SEQ, P_DIM), 1.0)
    p_sample = nrm(ks[3], (DEPTH, DEC_BATCH, DEC_SEQ, P_DIM), 1.0)
    cache_k = nrm(ks[4], (DEPTH, n_pool, PAGE_SIZE, 2 * ATT_HEADS, ATT_HD), 1.0)
    cache_v = nrm(ks[5], (DEPTH, n_pool, PAGE_SIZE, ATT_HEADS, 2 * ATT_HD), 1.0)
    perm = jax.random.permutation(ks[6], n_pool)
    page_table = perm[:n_used].reshape(DEC_BATCH, n_pages).astype(jnp.int32)
    state_ssm = nrm(ks[7], (DEPTH, DEC_BATCH, SSD_HEADS, SSD_HD, D_STATE), 0.1)
    state_conv = nrm(ks[8], (DEPTH, DEC_BATCH, CONV_W - 1, CONV_DIM), 1.0)
    col_scale = jnp.concatenate([jnp.ones((2 * ATT_WIDTH,), F32),
                                 jnp.full((ATT_WIDTH,), DEEPNORM_BETA, F32),
                                 jnp.ones((IN_DIM - 3 * ATT_WIDTH,), F32)])
    w_in = nrm(ks[9], (DEPTH, D_MODEL, IN_DIM), D_MODEL ** -0.5) * col_scale
    conv_w = nrm(ks[10], (DEPTH, CONV_W, CONV_DIM), CONV_W ** -0.5)
    conv_b = nrm(ks[11], (DEPTH, CONV_DIM), 0.01)
    dt0 = jnp.exp(jax.random.uniform(ks[12], (DEPTH, SSD_HEADS), F32)
                  * (math.log(DT_MAX) - math.log(DT_MIN)) + math.log(DT_MIN))
    dt_bias = dt0 + jnp.log(-jnp.expm1(-dt0))
    a_log = jnp.log(jax.random.uniform(ks[13], (DEPTH, SSD_HEADS), F32, 1.0, 16.0))
    d_skip = 1.0 + nrm(ks[14], (DEPTH, SSD_HEADS), 0.01)
    ssd_norm_w = 1.0 + nrm(ks[15], (DEPTH, SSD_INNER), 0.01)
    lambda_q1 = nrm(ks[16], (DEPTH, ATT_HD), 0.1)
    lambda_k1 = nrm(ks[17], (DEPTH, ATT_HD), 0.1)
    lambda_q2 = nrm(ks[18], (DEPTH, ATT_HD), 0.1)
    lambda_k2 = nrm(ks[19], (DEPTH, ATT_HD), 0.1)
    subln_w = 1.0 + nrm(ks[20], (DEPTH, 2 * ATT_HD), 0.01)
    w_out = nrm(ks[21], (DEPTH, D_MIX, D_MODEL), D_MIX ** -0.5 * DEEPNORM_BETA)
    ln1_g = 1.0 + nrm(ks[22], (DEPTH, D_MODEL), 0.01)
    ln1_b = nrm(ks[23], (DEPTH, D_MODEL), 0.01)
    w_ffn_in = nrm(ks[24], (DEPTH, D_MODEL, 2 * D_FF), D_MODEL ** -0.5 * DEEPNORM_BETA)
    w_ffn_out = nrm(ks[25], (DEPTH, D_FF, D_MODEL), D_FF ** -0.5 * DEEPNORM_BETA)
    w_pe = nrm(ks[26], (DEPTH, P_DIM, D_MODEL), P_DIM ** -0.5)
    w_pg = nrm(ks[27], (DEPTH, D_MODEL, D_MODEL), D_MODEL ** -0.5)
    ln2_g = 1.0 + nrm(ks[28], (DEPTH, D_MODEL), 0.01)
    ln2_b = nrm(ks[29], (DEPTH, D_MODEL), 0.01)
    return {"x_prompt": x_prompt, "x_sample": x_sample, "p_prompt": p_prompt, "p_sample": p_sample,
            "cache_k": cache_k, "cache_v": cache_v, "page_table": page_table,
            "state_ssm": state_ssm, "state_conv": state_conv,
            "w_in": w_in, "conv_w": conv_w, "conv_b": conv_b, "dt_bias": dt_bias, "a_log": a_log,
            "d_skip": d_skip, "ssd_norm_w": ssd_norm_w, "lambda_q1": lambda_q1, "lambda_k1": lambda_k1,
            "lambda_q2": lambda_q2, "lambda_k2": lambda_k2, "subln_w": subln_w, "w_out": w_out,
            "ln1_g": ln1_g, "ln1_b": ln1_b, "w_ffn_in": w_ffn_in, "w_ffn_out": w_ffn_out,
            "w_pe": w_pe, "w_pg": w_pg, "ln2_g": ln2_g, "ln2_b": ln2_b}


def reference(x_prompt, x_sample, p_prompt, p_sample, cache_k, cache_v, page_table, state_ssm, state_conv,
              w_in, conv_w, conv_b, dt_bias, a_log, d_skip, ssd_norm_w, lambda_q1, lambda_k1,
              lambda_q2, lambda_k2, subln_w, w_out, ln1_g, ln1_b, w_ffn_in, w_ffn_out,
              w_pe, w_pg, ln2_g, ln2_b):
    bp, lp, _ = x_prompt.shape
    bs, ls, _ = x_sample.shape
    n_pages = page_table.shape[1]
    past_len = n_pages * PAGE_SIZE
    pos_p = jnp.arange(lp)
    pos_s = past_len + jnp.arange(ls)
    hp, hs = x_prompt, x_sample
    kp_l, vp_l, sp_l, cp_l, ks_l, vs_l, ss_l, cs_l = [], [], [], [], [], [], [], []
    for i in range(DEPTH):
        prm = (w_in[i], conv_w[i], conv_b[i], dt_bias[i], a_log[i], d_skip[i], ssd_norm_w[i],
               lambda_q1[i], lambda_k1[i], lambda_q2[i], lambda_k2[i], subln_w[i], w_out[i],
               ln1_g[i], ln1_b[i], w_ffn_in[i], w_ffn_out[i], w_pe[i], w_pg[i], ln2_g[i], ln2_b[i])
        lam_init = 0.8 - 0.6 * math.exp(-0.3 * i)
        ssm0 = jnp.zeros((bp, SSD_HEADS, SSD_HD, D_STATE), F32)
        conv0 = jnp.zeros((bp, CONV_W - 1, CONV_DIM), x_prompt.dtype)
        hp, kp, vp, sp, cp = hybrid_layer(hp, p_prompt[i], pos_p, None, ssm0, conv0, prm, lam_init)
        past_k = cache_k[i][page_table].reshape(bs, past_len, 2 * ATT_HEADS, ATT_HD)
        past_v = cache_v[i][page_table].reshape(bs, past_len, ATT_HEADS, 2 * ATT_HD)
        hs, ks_, vs_, ss_, cs_ = hybrid_layer(hs, p_sample[i], pos_s, (past_k, past_v),
                                              state_ssm[i], state_conv[i], prm, lam_init)
        kp_l.append(kp); vp_l.append(vp); sp_l.append(sp); cp_l.append(cp)
        ks_l.append(ks_); vs_l.append(vs_); ss_l.append(ss_); cs_l.append(cs_)
    return (hp, hs, jnp.stack(kp_l), jnp.stack(vp_l), jnp.stack(sp_l), jnp.stack(cp_l),
            jnp.stack(ks_l), jnp.stack(vs_l), jnp.stack(ss_l), jnp.stack(cs_l))
```

```python
import functools
import math

import jax
import jax.numpy as jnp
from jax import lax
from jax.experimental import pallas as pl
from jax.experimental.pallas import tpu as pltpu

F32 = jnp.float32
BF16 = jnp.bfloat16

D_MODEL = 1024
ATT_HD = 64
ATT_HEADS = 8
ATT_WIDTH = 1024
ROT_DIM = 16
ROPE_THETA = 500000.0
SSD_HD = 64
SSD_INNER = 1024
SSD_HEADS = 16
SSD_GROUPS = 2
D_STATE = 128
CONV_W = 4
CONV_DIM = 1536
CHUNK = 128
D_FF = 2816
P_DIM = 256
EPS = 1e-5
LANES = 128
NEG = -1e30

VMEM_LIMIT = 56 * 1024 * 1024


def _cparams(sem):
    return pltpu.CompilerParams(dimension_semantics=sem, vmem_limit_bytes=VMEM_LIMIT)


def _dot(a, b):
    return jnp.dot(a, b, preferred_element_type=F32)


def _dot_nt(a, b):
    return lax.dot_general(a, b, (((1,), (1,)), ((), ())), preferred_element_type=F32)


def _split3(a):
    hi = a.astype(BF16)
    r = a - hi.astype(F32)
    mid = r.astype(BF16)
    lo = (r - mid.astype(F32)).astype(BF16)
    return hi, mid, lo


def _dot_sel_lhs(sel, a):
    hi, mid, lo = _split3(a)
    return _dot(sel, hi) + _dot(sel, mid) + _dot(sel, lo)


def _dot_sel_rhs(a, sel):
    hi, mid, lo = _split3(a)
    return _dot(hi, sel) + _dot(mid, sel) + _dot(lo, sel)


def _silu(x):
    return x * jax.nn.sigmoid(x)


def _softplus(x):
    return jnp.maximum(x, 0.0) + jnp.log1p(jnp.exp(-jnp.abs(x)))


def _rope_cols(t, ra, rb, rc):
    outs = []
    for j in range(t.shape[1] // LANES):
        c = t[:, j * LANES:(j + 1) * LANES]
        outs.append(c * ra + pltpu.roll(c, LANES - ROT_DIM // 2, 1) * rb + pltpu.roll(c, ROT_DIM // 2, 1) * rc)
    return outs


def _inproj_body(x_ref, wq_ref, wk_ref, wv_ref, wvt_ref, wz_ref, wxbc_ref, wdt_ref, ra_ref, rb_ref, rc_ref,
                 q_ref, kf_ref, vf_ref, z_ref, xbc_ref, dt_ref, *maybe_bf16, attn_copies):
    xb = x_ref[...].astype(BF16)
    ra = ra_ref[...]
    rb = rb_ref[...]
    rc = rc_ref[...]
    q = _dot(xb, wq_ref[...])
    for j, c in enumerate(_rope_cols(q, ra, rb, rc)):
        q_ref[:, j * LANES:(j + 1) * LANES] = (c * (ATT_HD ** -0.5)).astype(BF16)
    k = _dot(xb, wk_ref[...])
    for j, c in enumerate(_rope_cols(k, ra, rb, rc)):
        kf_ref[:, j * LANES:(j + 1) * LANES] = c
        if attn_copies:
            maybe_bf16[0][:, j * LANES:(j + 1) * LANES] = c.astype(BF16)
    vf_ref[...] = _dot(xb, wv_ref[...])
    if attn_copies:
        maybe_bf16[1][0, 0] = _dot_nt(wvt_ref[...], xb).astype(BF16)
    z_ref[...] = _dot(xb, wz_ref[...])
    xbc_ref[...] = _dot(xb, wxbc_ref[...])
    dt_ref[...] = _dot(xb, wdt_ref[...])


def _inproj(x2d, w, tabs, *, tm, nbatch, attn_copies):
    T = x2d.shape[0]
    nt = T // tm
    ntab = tabs[0].shape[0] // tm
    per_b = nt // nbatch
    full = lambda shape: pl.BlockSpec(shape, lambda i: (0,) * len(shape))
    row = lambda width: pl.BlockSpec((tm, width), lambda i: (i, 0))
    tab = pl.BlockSpec((tm, LANES), lambda i: (i % ntab, 0))
    in_specs = [row(D_MODEL), full((D_MODEL, 1024)), full((D_MODEL, 1024)), full((D_MODEL, 1024)),
                full((1024, D_MODEL)), full((D_MODEL, 1024)), full((D_MODEL, CONV_DIM)), full((D_MODEL, LANES)),
                tab, tab, tab]
    out_shape = [jax.ShapeDtypeStruct((T, 1024), BF16), jax.ShapeDtypeStruct((T, 1024), F32),
                 jax.ShapeDtypeStruct((T, 1024), F32), jax.ShapeDtypeStruct((T, 1024), F32),
                 jax.ShapeDtypeStruct((T, CONV_DIM), F32), jax.ShapeDtypeStruct((T, LANES), F32)]
    out_specs = [row(1024), row(1024), row(1024), row(1024), row(CONV_DIM), row(LANES)]
    if attn_copies:
        out_shape += [jax.ShapeDtypeStruct((T, 1024), BF16),
                      jax.ShapeDtypeStruct((nbatch, per_b, 1024, tm), BF16)]
        out_specs += [row(1024), pl.BlockSpec((1, 1, 1024, tm), lambda i: (i // per_b, i % per_b, 0, 0))]
    return pl.pallas_call(
        functools.partial(_inproj_body, attn_copies=attn_copies),
        grid=(nt,), in_specs=in_specs, out_specs=out_specs, out_shape=out_shape,
        compiler_params=_cparams(("parallel",)), name="inproj",
    )(x2d, w["wq"], w["wk"], w["wv"], w["wvt"], w["wz"], w["wxbc"], w["wdt"], *tabs)


def _ssd_chunk_math(xpad_ref, dt_raw, lmat, mtot, e64, e128, convw, convb, dtb, alog):
    acc = convb
    for j in range(CONV_W):
        acc = acc + xpad_ref[pl.ds(8 - (CONV_W - 1) + j, CHUNK), :] * convw[j:j + 1, :]
    act = _silu(acc)
    xs = act[:, :SSD_INNER]
    lane = lax.broadcasted_iota(jnp.int32, (1, LANES), 1)
    a_row = jnp.where(lane < SSD_HEADS, -jnp.exp(alog), 0.0)
    dtv = _softplus(dt_raw + dtb)
    dA = dtv * a_row
    cs = _dot_sel_lhs(lmat, dA)
    cst = _dot_sel_lhs(mtot, dA)
    csT = cs.T
    dt_e = _dot_sel_rhs(dtv, e64)
    cs_e = _dot_sel_rhs(cs, e64)
    cst_e = _dot_sel_rhs(cst, e64)
    cs_b = _dot_sel_rhs(cs, e128)
    X = xs * dt_e
    Xd = X * jnp.exp(cst_e - cs_e)
    ecs_e = jnp.exp(cs_e)
    lbool = lmat.astype(F32) > 0.5
    lane2 = lax.broadcasted_iota(jnp.int32, (CHUNK, LANES), 1)
    ydiag = []
    bgs, cgs = [], []
    for g in range(SSD_GROUPS):
        bg = act[:, SSD_INNER + g * D_STATE:SSD_INNER + (g + 1) * D_STATE].astype(BF16)
        cg = act[:, SSD_INNER + (SSD_GROUPS + g) * D_STATE:SSD_INNER + (SSD_GROUPS + g + 1) * D_STATE].astype(BF16)
        bgs.append(bg)
        cgs.append(cg)
        cb = _dot_nt(cg, bg)
        for jj in range(SSD_HEADS // SSD_GROUPS // 2):
            h0 = g * (SSD_HEADS // SSD_GROUPS) + 2 * jj
            xp = X[:, h0 * SSD_HD:h0 * SSD_HD + LANES].astype(BF16)
            yp = []
            for h in (h0, h0 + 1):
                seg = cs_b[:, h * LANES:(h + 1) * LANES] - csT[h:h + 1, :]
                lm = jnp.exp(jnp.where(lbool, seg, NEG))
                yp.append(_dot((cb * lm).astype(BF16), xp))
            ydiag.append(jnp.where(lane2 < SSD_HD, yp[0], yp[1]))
    return dict(act=act, xs=xs, Xd=Xd, ecs_e=ecs_e, csT=csT, ydiag=ydiag, bgs=bgs, cgs=cgs)


def _ssd_finish(y_cols, xs, z, dskip, normw, out_ref):
    gz = []
    for j in range(SSD_INNER // LANES):
        sl = slice(j * LANES, (j + 1) * LANES)
        y = y_cols[j] + dskip[:, sl] * xs[:, sl]
        gz.append(y * _silu(z[:, sl]))
    per_g = SSD_INNER // SSD_GROUPS // LANES
    for g in range(SSD_GROUPS):
        blk = gz[g * per_g:(g + 1) * per_g]
        ss = sum(jnp.sum(b * b, axis=1, keepdims=True) for b in blk)
        inv = lax.rsqrt(ss / (SSD_INNER // SSD_GROUPS) + EPS)
        for j, b in enumerate(blk):
            sl = slice((g * per_g + j) * LANES, (g * per_g + j + 1) * LANES)
            out_ref[:, sl] = (b * inv * normw[:, sl]).astype(out_ref.dtype)


def _ssd_prompt_body(xbc_ref, dt_ref, z_ref, lmat_ref, mtot_ref, e64_ref, e128_ref, convw_ref, convb_ref,
                     dtb_ref, alog_ref, dskip_ref, normw_ref, out_ref, state_ref, xpad_ref):
    c = pl.program_id(1)

    @pl.when(c == 0)
    def _():
        xpad_ref[pl.ds(0, 8), :] = jnp.zeros((8, CONV_DIM), F32)
        state_ref[...] = jnp.zeros(state_ref.shape, F32)

    xpad_ref[pl.ds(8, CHUNK), :] = xbc_ref[...]
    m = _ssd_chunk_math(xpad_ref, dt_ref[...], lmat_ref[...], mtot_ref[...], e64_ref[...], e128_ref[...],
                        convw_ref[...], convb_ref[...], dtb_ref[...], alog_ref[...])
    xpad_ref[pl.ds(8 - (CONV_W - 1), CONV_W - 1), :] = xpad_ref[pl.ds(8 + CHUNK - (CONV_W - 1), CONV_W - 1), :]

    csT = m["csT"]
    cd = jnp.exp(jnp.broadcast_to(csT[0:SSD_HEADS, CHUNK - 1:CHUNK], (SSD_HEADS, LANES)))
    hpg = SSD_HEADS // SSD_GROUPS
    gw = hpg * SSD_HD
    y_cols = []
    for g in range(SSD_GROUPS):
        sg = state_ref[0, pl.ds(g * gw, gw), :]
        yoff = _dot_nt(m["cgs"][g], sg.astype(BF16))
        for jj in range(gw // LANES):
            col = g * (gw // LANES) + jj
            y_cols.append(m["ydiag"][col] + yoff[:, jj * LANES:(jj + 1) * LANES]
                          * m["ecs_e"][:, col * LANES:(col + 1) * LANES])
        xdg_t = m["Xd"][:, g * gw:(g + 1) * gw].T.astype(BF16)
        contrib = _dot(xdg_t, m["bgs"][g])
        for hh in range(hpg):
            h = g * hpg + hh
            rows = pl.ds(h * SSD_HD, SSD_HD)
            state_ref[0, rows, :] = (state_ref[0, rows, :] * cd[h:h + 1, :]
                                     + contrib[hh * SSD_HD:(hh + 1) * SSD_HD, :])
    _ssd_finish(y_cols, m["xs"], z_ref[...], dskip_ref[...], normw_ref[...], out_ref)


def _ssd_consts(p):
    return [p["e64"], p["e128"], p["conv_w"], p["conv_b"], p["dt_bias"], p["a_log"], p["d_skip"], p["norm_w"]]


def _const_specs(nd_grid):
    z = (0, 0)
    if nd_grid == 2:
        f = lambda shape: pl.BlockSpec(shape, lambda b, c: z)
    else:
        f = lambda shape: pl.BlockSpec(shape, lambda i: z)
    return [f((LANES, SSD_INNER)), f((LANES, SSD_HEADS * LANES)), f((CONV_W, CONV_DIM)), f((1, CONV_DIM)),
            f((1, LANES)), f((1, LANES)), f((1, SSD_INNER)), f((1, SSD_INNER))]


def _ssd_prompt(xbc, dt, z, p, nbatch):
    T = xbc.shape[0]
    nc = T // nbatch // CHUNK
    row = lambda width: pl.BlockSpec((CHUNK, width), lambda b, c: (b * nc + c, 0))
    sq = pl.BlockSpec((CHUNK, CHUNK), lambda b, c: (0, 0))
    tri = jnp.tril(jnp.ones((CHUNK, CHUNK), F32)).astype(BF16)
    ones = jnp.ones((CHUNK, CHUNK), BF16)
    return pl.pallas_call(
        _ssd_prompt_body,
        grid=(nbatch, nc),
        in_specs=[row(CONV_DIM), row(LANES), row(SSD_INNER), sq, sq] + _const_specs(2),
        out_specs=[row(SSD_INNER), pl.BlockSpec((1, SSD_HEADS * SSD_HD, D_STATE), lambda b, c: (b, 0, 0))],
        out_shape=[jax.ShapeDtypeStruct((T, SSD_INNER), BF16),
                   jax.ShapeDtypeStruct((nbatch, SSD_HEADS * SSD_HD, D_STATE), F32)],
        scratch_shapes=[pltpu.VMEM((8 + CHUNK, CONV_DIM), F32)],
        compiler_params=_cparams(("parallel", "arbitrary")), name="ssd_prompt",
    )(xbc, dt, z, tri, ones, *_ssd_consts(p))


SROWS = 8
SB = CHUNK // SROWS


def _ssd_sample_body(xbc_ref, dt_ref, z_ref, s0_ref, lmat_ref, mtot_ref, e64_ref, e128_ref, convw_ref, convb_ref,
                     dtb_ref, alog_ref, dskip_ref, normw_ref, out_ref, s1_ref, xpad_ref):
    xpad_ref[pl.ds(0, 8), :] = jnp.zeros((8, CONV_DIM), F32)
    xpad_ref[pl.ds(8, CHUNK), :] = xbc_ref[...]
    m = _ssd_chunk_math(xpad_ref, dt_ref[...], lmat_ref[...], mtot_ref[...], e64_ref[...], e128_ref[...],
                        convw_ref[...], convb_ref[...], dtb_ref[...], alog_ref[...])
    csT = m["csT"]
    hpg = SSD_HEADS // SSD_GROUPS
    gw = hpg * SSD_HD
    rowi = lax.broadcasted_iota(jnp.int32, (CHUNK, LANES), 0)
    last_tok = CONV_W - 1 + DEC_Q - 1
    xdT = [m["Xd"][:, g * gw:(g + 1) * gw].T.astype(BF16) for g in range(SSD_GROUPS)]
    yoff = [[jnp.zeros((CHUNK, LANES), F32) for _ in range(gw // LANES)] for _ in range(SSD_GROUPS)]
    for b in range(SB):
        lo = b * SROWS + (CONV_W - 1)
        tok = (rowi >= lo) & (rowi <= b * SROWS + last_tok)
        col = b * SROWS + last_tok
        cd = jnp.exp(jnp.broadcast_to(csT[0:SSD_HEADS, col:col + 1], (SSD_HEADS, LANES)))
        for g in range(SSD_GROUPS):
            sg = s0_ref[b, pl.ds(g * gw, gw), :]
            res = _dot_nt(m["cgs"][g], sg.astype(BF16))
            for jj in range(gw // LANES):
                yoff[g][jj] = jnp.where(tok, res[:, jj * LANES:(jj + 1) * LANES], yoff[g][jj])
            bmask = jnp.where(tok, m["bgs"][g], jnp.zeros_like(m["bgs"][g]))
            contrib = _dot(xdT[g], bmask)
            for hh in range(hpg):
                h = g * hpg + hh
                rows = pl.ds(h * SSD_HD, SSD_HD)
                s1_ref[b, rows, :] = (s0_ref[b, rows, :] * cd[h:h + 1, :]
                                      + contrib[hh * SSD_HD:(hh + 1) * SSD_HD, :])
    y_cols = []
    for g in range(SSD_GROUPS):
        for jj in range(gw // LANES):
            col = g * (gw // LANES) + jj
            y_cols.append(m["ydiag"][col] + yoff[g][jj] * m["ecs_e"][:, col * LANES:(col + 1) * LANES])
    _ssd_finish(y_cols, m["xs"], z_ref[...], dskip_ref[...], normw_ref[...], out_ref)


def _ssd_sample(xbc8, dt8, z8, s0, p):
    T8 = xbc8.shape[0]
    nblk = T8 // CHUNK
    row = lambda width: pl.BlockSpec((CHUNK, width), lambda i: (i, 0))
    sq = pl.BlockSpec((CHUNK, CHUNK), lambda i: (0, 0))
    st = pl.BlockSpec((SB, SSD_HEADS * SSD_HD, D_STATE), lambda i: (i, 0, 0))
    r = jnp.arange(CHUNK)
    same = (r[:, None] // SROWS) == (r[None, :] // SROWS)
    is_tok = ((r % SROWS) >= CONV_W - 1) & ((r % SROWS) < SROWS - 1)
    lmat = (same & (r[None, :] <= r[:, None]) & is_tok[None, :] & is_tok[:, None]).astype(BF16)
    mtot = (same & is_tok[None, :]).astype(BF16)
    return pl.pallas_call(
        _ssd_sample_body,
        grid=(nblk,),
        in_specs=[row(CONV_DIM), row(LANES), row(SSD_INNER), st, sq, sq] + _const_specs(1),
        out_specs=[row(SSD_INNER), st],
        out_shape=[jax.ShapeDtypeStruct((T8, SSD_INNER), BF16), jax.ShapeDtypeStruct(s0.shape, F32)],
        scratch_shapes=[pltpu.VMEM((8 + CHUNK, CONV_DIM), F32)],
        compiler_params=_cparams(("parallel",)), name="ssd_sample",
    )(xbc8, dt8, z8, s0, lmat, mtot, *_ssd_consts(p))


def _lam_value(lq1, lk1, lq2, lk2, lam_init):
    s1 = jnp.sum(lq1[...] * lk1[...], axis=1, keepdims=True)
    s2 = jnp.sum(lq2[...] * lk2[...], axis=1, keepdims=True)
    return jnp.exp(s1) - jnp.exp(s2) + lam_init


def _flash_body(q_ref, k_ref, vt_ref, lq1, lk1, lq2, lk2, sw_ref, o_ref, m_ref, l_ref, acc_ref, *, tq, lam_init):
    i = pl.program_id(2)
    tk = tq
    q = q_ref[...]
    lane = lax.broadcasted_iota(jnp.int32, (tq, LANES), 1)
    zero = jnp.zeros_like(q)
    qs = jnp.concatenate([jnp.where(lane < ATT_HD, q, zero), jnp.where(lane >= ATT_HD, q, zero)], axis=0)
    m_ref[...] = jnp.full(m_ref.shape, NEG, F32)
    l_ref[...] = jnp.zeros(l_ref.shape, F32)
    acc_ref[...] = jnp.zeros(acc_ref.shape, F32)

    def step(j, masked):
        kblk = k_ref[pl.ds(pl.multiple_of(j * tk, tk), tk), :]
        s = _dot_nt(kblk, qs)
        if masked:
            kpos = lax.broadcasted_iota(jnp.int32, (tk, 2 * tq), 0)
            qpos = lax.broadcasted_iota(jnp.int32, (tk, 2 * tq), 1)
            qpos = jnp.where(qpos >= tq, qpos - tq, qpos)
            s = jnp.where(kpos <= qpos, s, NEG)
        m_old = m_ref[...]
        m_new = jnp.maximum(m_old, jnp.max(s, axis=0, keepdims=True))
        alpha = jnp.exp(m_old - m_new)
        p = jnp.exp(s - m_new)
        l_ref[...] = alpha * l_ref[...] + jnp.sum(p, axis=0, keepdims=True)
        m_ref[...] = m_new
        acc_ref[...] = alpha * acc_ref[...] + _dot(vt_ref[0, j], p.astype(BF16))

    def body(j, carry):
        step(j, False)
        return carry

    lax.fori_loop(0, i, body, 0)
    step(i, True)

    lam = _lam_value(lq1, lk1, lq2, lk2, lam_init)
    acc = acc_ref[...]
    l = l_ref[...]
    o = acc[:, :tq] / l[:, :tq] - lam * (acc[:, tq:] / l[:, tq:])
    inv = lax.rsqrt(jnp.mean(o * o, axis=0, keepdims=True) + EPS)
    o_ref[...] = ((o * inv).T * (sw_ref[...] * (1.0 - lam_init))).astype(o_ref.dtype)


def _flash(q, kb, vt, lams, subln, *, tq, lam_init):
    T = q.shape[0]
    nb, nk = vt.shape[0], vt.shape[1]
    L = T // nb
    nq = L // tq
    vec = pl.BlockSpec((1, ATT_HD), lambda b, h, i: (0, 0))
    return pl.pallas_call(
        functools.partial(_flash_body, tq=tq, lam_init=lam_init),
        grid=(nb, ATT_HEADS, nq),
        in_specs=[pl.BlockSpec((tq, LANES), lambda b, h, i: (b * nq + i, h)),
                  pl.BlockSpec((L, LANES), lambda b, h, i: (b, h)),
                  pl.BlockSpec((1, nk, LANES, tq), lambda b, h, i: (b, 0, h, 0)),
                  vec, vec, vec, vec, pl.BlockSpec((1, LANES), lambda b, h, i: (0, 0))],
        out_specs=pl.BlockSpec((tq, LANES), lambda b, h, i: (b * nq + i, h)),
        out_shape=jax.ShapeDtypeStruct((T, ATT_WIDTH), BF16),
        scratch_shapes=[pltpu.VMEM((1, 2 * tq), F32), pltpu.VMEM((1, 2 * tq), F32),
                        pltpu.VMEM((LANES, 2 * tq), F32)],
        compiler_params=_cparams(("parallel", "parallel", "arbitrary")), name="flash_diff",
    )(q, kb, vt, *lams, subln)


PAGES_PER_STEP = 4
DEC_Q = 4
DROWS = ATT_HEADS * DEC_Q


def _decode_body(pt_ref, qe_ref, qo_ref, kn_ref, vn_ref, *refs, pps, page, lam_init):
    k_refs = refs[:pps]
    v_refs = refs[pps:2 * pps]
    lq1, lk1, lq2, lk2, sw_ref, o_ref, m_ref, l_ref, acc_ref = refs[2 * pps:]
    j = pl.program_id(1)
    nsteps = pl.num_programs(1)
    nrow = page * ATT_HEADS

    @pl.when(j == 0)
    def _():
        m_ref[...] = jnp.full(m_ref.shape, NEG, F32)
        l_ref[...] = jnp.zeros(l_ref.shape, F32)
        acc_ref[...] = jnp.zeros(acc_ref.shape, F32)

    qs = (qe_ref[0], qo_ref[0])

    def update(c, s, v2, valid):
        s = jnp.where(valid, s, NEG)
        m_old = m_ref[c]
        m_new = jnp.maximum(m_old, jnp.max(s, axis=1, keepdims=True))
        alpha = jnp.exp(m_old - m_new)
        p = jnp.where(valid, jnp.exp(s - m_new), 0.0)
        l_ref[c] = alpha * l_ref[c] + jnp.sum(p, axis=1, keepdims=True)
        m_ref[c] = m_new
        acc_ref[c] = alpha * acc_ref[c] + _dot(p.astype(BF16), v2)

    ri = lax.broadcasted_iota(jnp.int32, (DROWS, nrow), 0)
    li = lax.broadcasted_iota(jnp.int32, (DROWS, nrow), 1)
    same_head = (li % ATT_HEADS) == (ri // DEC_Q)
    for i in range(pps):
        v2 = v_refs[i][0].astype(BF16)
        for c in range(2):
            k2 = k_refs[i][0, pl.ds(c, nrow, stride=2), :].astype(BF16)
            update(c, _dot_nt(qs[c], k2), v2, same_head)

    @pl.when(j == nsteps - 1)
    def _():
        nnew = DEC_Q * ATT_HEADS
        r2 = lax.broadcasted_iota(jnp.int32, (DROWS, nnew), 0)
        l2 = lax.broadcasted_iota(jnp.int32, (DROWS, nnew), 1)
        valid = ((l2 % ATT_HEADS) == (r2 // DEC_Q)) & ((l2 // ATT_HEADS) <= (r2 % DEC_Q))
        vn = vn_ref[0].astype(BF16)
        for c in range(2):
            k2 = kn_ref[0, pl.ds(c, nnew, stride=2), :].astype(BF16)
            update(c, _dot_nt(qs[c], k2), vn, valid)
        lam = _lam_value(lq1, lk1, lq2, lk2, lam_init)
        o = acc_ref[0] / l_ref[0] - lam * (acc_ref[1] / l_ref[1])
        inv = lax.rsqrt(jnp.mean(o * o, axis=1, keepdims=True) + EPS)
        o_ref[0] = (o * inv * (sw_ref[...] * (1.0 - lam_init))).astype(o_ref.dtype)


def _decode_attn(qe, qo, knew, vnew, ck, cv, page_table, lams, subln, *, lam_init):
    nb, npages = page_table.shape
    page = cv.shape[1] // ATT_HEADS
    pps = PAGES_PER_STEP
    nsteps = npages // pps

    def page_spec(shape, i):
        return pl.BlockSpec(shape, lambda b, j, pt: (pt[b, j * pps + i], 0, 0))

    per_b = lambda shape: pl.BlockSpec(shape, lambda b, j, pt: (b, 0, 0))
    vec = pl.BlockSpec((1, ATT_HD), lambda b, j, pt: (0, 0))
    in_specs = ([per_b((1, DROWS, ATT_HD)), per_b((1, DROWS, ATT_HD)),
                 per_b((1, DEC_Q * 2 * ATT_HEADS, ATT_HD)), per_b((1, DEC_Q * ATT_HEADS, LANES))]
                + [page_spec((1, page * 2 * ATT_HEADS, ATT_HD), i) for i in range(pps)]
                + [page_spec((1, page * ATT_HEADS, LANES), i) for i in range(pps)]
                + [vec, vec, vec, vec, pl.BlockSpec((1, LANES), lambda b, j, pt: (0, 0))])
    grid_spec = pltpu.PrefetchScalarGridSpec(
        num_scalar_prefetch=1, grid=(nb, nsteps), in_specs=in_specs,
        out_specs=per_b((1, DROWS, LANES)),
        scratch_shapes=[pltpu.VMEM((2, DROWS, 1), F32), pltpu.VMEM((2, DROWS, 1), F32),
                        pltpu.VMEM((2, DROWS, LANES), F32)])
    return pl.pallas_call(
        functools.partial(_decode_body, pps=pps, page=page, lam_init=lam_init),
        grid_spec=grid_spec,
        out_shape=jax.ShapeDtypeStruct((nb, DROWS, LANES), BF16),
        compiler_params=_cparams(("parallel", "arbitrary")), name="decode_attn",
    )(page_table, qe, qo, knew, vnew, *([ck] * pps), *([cv] * pps), *lams, subln)


def _layer_norm(x, g, b):
    mu = jnp.mean(x, axis=1, keepdims=True)
    xc = x - mu
    var = jnp.mean(xc * xc, axis=1, keepdims=True)
    return xc * lax.rsqrt(var + EPS) * g + b


def _post_body(att_ref, ssd_ref, x_ref, pe_ref, woa_ref, wos_ref, wg_ref, wu_ref, wfo_ref, wpe_ref, wpg_ref,
               g1_ref, b1_ref, g2_ref, b2_ref, o_ref, *, alpha, nff):
    mix = _dot(att_ref[...], woa_ref[...]) + _dot(ssd_ref[...], wos_ref[...])
    h = _layer_norm(alpha * x_ref[...] + mix, g1_ref[...], b1_ref[...])
    hb = h.astype(BF16)
    ffn = None
    step = D_FF // nff
    for c in range(nff):
        sl = slice(c * step, (c + 1) * step)
        gate = _dot(hb, wg_ref[:, sl])
        up = _dot(hb, wu_ref[:, sl])
        part = _dot((_silu(gate) * up).astype(BF16), wfo_ref[sl, :])
        ffn = part if ffn is None else ffn + part
    pemb = _dot(pe_ref[...].astype(BF16), wpe_ref[...]) * jax.nn.sigmoid(_dot(hb, wpg_ref[...]))
    o_ref[...] = _layer_norm(alpha * h + ffn + pemb, g2_ref[...], b2_ref[...])


def _post(att, ssd, x2d, pe2d, w, *, tm, alpha):
    T = x2d.shape[0]
    row = lambda width: pl.BlockSpec((tm, width), lambda i: (i, 0))
    full = lambda shape: pl.BlockSpec(shape, lambda i: (0, 0), pipeline_mode=pl.Buffered(1))
    return pl.pallas_call(
        functools.partial(_post_body, alpha=alpha, nff=2),
        grid=(T // tm,),
        in_specs=[row(1024), row(1024), row(D_MODEL), row(P_DIM),
                  full((1024, D_MODEL)), full((1024, D_MODEL)), full((D_MODEL, D_FF)), full((D_MODEL, D_FF)),
                  full((D_FF, D_MODEL)), full((P_DIM, D_MODEL)), full((D_MODEL, D_MODEL)),
                  full((1, D_MODEL)), full((1, D_MODEL)), full((1, D_MODEL)), full((1, D_MODEL))],
        out_specs=row(D_MODEL),
        out_shape=jax.ShapeDtypeStruct((T, D_MODEL), F32),
        compiler_params=_cparams(("parallel",)), name="post",
    )(att, ssd, x2d, pe2d, w["woa"], w["wos"], w["wg"], w["wu"], w["wfo"], w["wpe"], w["wpg"],
      w["g1"], w["b1"], w["g2"], w["b2"])


def _rope_tables(pos):
    half = ROT_DIM // 2
    inv = ROPE_THETA ** (-jnp.arange(half, dtype=F32) * 2.0 / ROT_DIM)
    ang = pos.astype(F32)[:, None] * inv[None, :]
    cos, sin = jnp.cos(ang), jnp.sin(ang)
    n = pos.shape[0]
    pad = jnp.zeros((n, ATT_HD - ROT_DIM), F32)
    ra = jnp.concatenate([cos, cos, jnp.ones((n, ATT_HD - ROT_DIM), F32)], 1)
    rb = jnp.concatenate([-sin, jnp.zeros((n, half), F32), pad], 1)
    rc = jnp.concatenate([jnp.zeros((n, half), F32), sin, pad], 1)
    rep = LANES // ATT_HD
    return tuple(jnp.tile(t, (1, rep)) for t in (ra, rb, rc))


def _layer(depth, lam_init, x_prompt, x_sample, p_prompt, p_sample, cache_k, cache_v, page_table, state_ssm,
           state_conv, prm):
    (w_in, conv_w, conv_b, dt_bias, a_log, d_skip, ssd_norm_w, lq1, lk1, lq2, lk2, subln_w,
     w_out, ln1_g, ln1_b, w_ffn_in, w_ffn_out, w_pe, w_pg, ln2_g, ln2_b) = prm
    bp, lp, _ = x_prompt.shape
    bs, ls, _ = x_sample.shape
    npages = page_table.shape[1]
    page = cache_k.shape[1]
    past_len = npages * page
    alpha = (2 * depth) ** 0.25

    wb = w_in.astype(BF16)
    wq = wb[:, 0:1024]
    wk = wb[:, 1024:2048]
    wv = wb[:, 2048:3072]
    wz = wb[:, 3072:4096]
    wxbc = wb[:, 4096:4096 + CONV_DIM]
    wdt = jnp.pad(wb[:, 4096 + CONV_DIM:], ((0, 0), (0, LANES - SSD_HEADS)))
    w1 = dict(wq=wq, wk=wk, wv=wv, wvt=wv.T, wz=wz, wxbc=wxbc, wdt=wdt)

    hrep = jnp.arange(SSD_HEADS * SSD_HD) // SSD_HD
    e64 = (jnp.arange(LANES)[:, None] == hrep[None, :]).astype(BF16)
    e128 = (jnp.arange(LANES)[:, None] == (jnp.arange(SSD_HEADS * LANES) // LANES)[None, :]).astype(BF16)
    padl = lambda v: jnp.pad(v.astype(F32), (0, LANES - v.shape[0]))[None, :]
    pssd = dict(e64=e64, e128=e128, conv_w=conv_w.astype(F32), conv_b=conv_b.astype(F32)[None, :],
                dt_bias=padl(dt_bias), a_log=padl(a_log), d_skip=jnp.repeat(d_skip.astype(F32), SSD_HD)[None, :],
                norm_w=ssd_norm_w.astype(F32)[None, :])
    lams = [v.astype(F32)[None, :] for v in (lq1, lk1, lq2, lk2)]
    subln = subln_w.astype(F32)[None, :]
    wob = w_out.astype(BF16)
    wfi = w_ffn_in.astype(BF16)
    w4 = dict(woa=wob[:ATT_WIDTH], wos=wob[ATT_WIDTH:], wg=wfi[:, :D_FF], wu=wfi[:, D_FF:],
              wfo=w_ffn_out.astype(BF16), wpe=w_pe.astype(BF16), wpg=w_pg.astype(BF16),
              g1=ln1_g.astype(F32)[None, :], b1=ln1_b.astype(F32)[None, :],
              g2=ln2_g.astype(F32)[None, :], b2=ln2_b.astype(F32)[None, :])

    tm = 256
    xp2 = x_prompt.reshape(bp * lp, D_MODEL)
    q, kf, vf, z, xbc, dt, kb, vt = _inproj(xp2, w1, _rope_tables(jnp.arange(lp)), tm=tm, nbatch=bp,
                                            attn_copies=True)
    ssd_p, ssm_p = _ssd_prompt(xbc, dt, z, pssd, bp)
    att_p = _flash(q, kb, vt, lams, subln, tq=tm, lam_init=lam_init)
    y_p = _post(att_p, ssd_p, xp2, p_prompt.reshape(bp * lp, P_DIM), w4, tm=tm, alpha=alpha)
    conv_p = xbc.reshape(bp, lp, CONV_DIM)[:, lp - (CONV_W - 1):, :]

    hist = CONV_W - 1
    x8 = jnp.zeros((bs, SROWS, D_MODEL), F32).at[:, hist:hist + ls].set(x_sample)
    pos8 = jnp.tile(jnp.clip(jnp.arange(SROWS) - hist, 0, ls - 1) + past_len, bs)
    qs, kfs, vfs, zs, xbcs, dts = _inproj(x8.reshape(bs * SROWS, D_MODEL), w1, _rope_tables(pos8),
                                          tm=256, nbatch=1, attn_copies=False)
    xbc8 = xbcs.reshape(bs, SROWS, CONV_DIM).at[:, :hist].set(state_conv.astype(F32))
    ssd_s8, ssm_s = _ssd_sample(xbc8.reshape(bs * SROWS, CONV_DIM), dts, zs,
                                state_ssm.reshape(bs, SSD_HEADS * SSD_HD, D_STATE), pssd)
    tok = lambda a, w: a.reshape(bs, SROWS, w)[:, hist:hist + ls]
    q4 = tok(qs, 1024).reshape(bs, ls, 2 * ATT_HEADS, ATT_HD)
    qc = jnp.transpose(q4.reshape(bs, ls, ATT_HEADS, 2, ATT_HD), (3, 0, 2, 1, 4))
    qe = qc[0].reshape(bs, DROWS, ATT_HD)
    qo = qc[1].reshape(bs, DROWS, ATT_HD)
    k_s = tok(kfs, 1024)
    v_s = tok(vfs, 1024)
    att_s = _decode_attn(qe, qo, k_s.reshape(bs, ls * 2 * ATT_HEADS, ATT_HD), v_s.reshape(bs, ls * ATT_HEADS, LANES),
                         cache_k.reshape(cache_k.shape[0], page * 2 * ATT_HEADS, ATT_HD),
                         cache_v.reshape(cache_v.shape[0], page * ATT_HEADS, 2 * ATT_HD),
                         page_table, lams, subln, lam_init=lam_init)
    att_s = jnp.transpose(att_s.reshape(bs, ATT_HEADS, ls, LANES), (0, 2, 1, 3)).reshape(bs * ls, ATT_WIDTH)
    ssd_s = tok(ssd_s8, SSD_INNER).reshape(bs * ls, SSD_INNER)
    y_s = _post(att_s, ssd_s, x_sample.reshape(bs * ls, D_MODEL), p_sample.reshape(bs * ls, P_DIM), w4,
                tm=min(256, bs * ls), alpha=alpha)
    conv_s = tok(xbcs, CONV_DIM)[:, ls - hist:]

    return (y_p.reshape(bp, lp, D_MODEL), y_s.reshape(bs, ls, D_MODEL),
            kf.reshape(bp, lp, 2 * ATT_HEADS, ATT_HD), vf.reshape(bp, lp, ATT_HEADS, 2 * ATT_HD),
            ssm_p.reshape(bp, SSD_HEADS, SSD_HD, D_STATE), conv_p,
            k_s.reshape(bs, ls, 2 * ATT_HEADS, ATT_HD), v_s.reshape(bs, ls, ATT_HEADS, 2 * ATT_HD),
            ssm_s.reshape(bs, SSD_HEADS, SSD_HD, D_STATE), conv_s)


def kernel(x_prompt, x_sample, p_prompt, p_sample, cache_k, cache_v, page_table, state_ssm, state_conv, w_in, conv_w, conv_b, dt_bias, a_log, d_skip, ssd_norm_w, lambda_q1, lambda_k1, lambda_q2, lambda_k2, subln_w, w_out, ln1_g, ln1_b, w_ffn_in, w_ffn_out, w_pe, w_pg, ln2_g, ln2_b):
    depth = w_in.shape[0]
    assert depth == 1, "single-layer trunk"
    prm = tuple(a[0] for a in (w_in, conv_w, conv_b, dt_bias, a_log, d_skip, ssd_norm_w, lambda_q1, lambda_k1,
                               lambda_q2, lambda_k2, subln_w, w_out, ln1_g, ln1_b, w_ffn_in, w_ffn_out, w_pe,
                               w_pg, ln2_g, ln2_b))
    lam_init = 0.8 - 0.6 * math.exp(-0.3 * 0)
    outs = _layer(depth, lam_init, x_prompt, x_sample, p_prompt[0], p_sample[0], cache_k[0], cache_v[0], page_table,
                  state_ssm[0], state_conv[0], prm)
    y_p, y_s = outs[0], outs[1]
    return (y_p, y_s) + tuple(o[None] for o in outs[2:])
```

```python
import functools
import math

import jax
import jax.numpy as jnp
from jax import lax
from jax.experimental import pallas as pl
from jax.experimental.pallas import tpu as pltpu

F32 = jnp.float32
BF16 = jnp.bfloat16

D_MODEL = 1024
ATT_HD = 64
ATT_HEADS = 8
ATT_WIDTH = 1024
ROT_DIM = 16
ROPE_THETA = 500000.0
SSD_HD = 64
SSD_INNER = 1024
SSD_HEADS = 16
SSD_GROUPS = 2
D_STATE = 128
CONV_W = 4
CONV_DIM = 1536
CHUNK = 128
D_FF = 2816
P_DIM = 256
EPS = 1e-5
LANES = 128
NEG = -1e30
Q_SCALE = (ATT_HD ** -0.5) * math.log2(math.e)

VMEM_LIMIT = 56 * 1024 * 1024


def _cparams(sem):
    return pltpu.CompilerParams(dimension_semantics=sem, vmem_limit_bytes=VMEM_LIMIT)


def _dot(a, b):
    return jnp.dot(a, b, preferred_element_type=F32)


def _dot_nt(a, b):
    return lax.dot_general(a, b, (((1,), (1,)), ((), ())), preferred_element_type=F32)


def _split3(a):
    hi = a.astype(BF16)
    r = a - hi.astype(F32)
    mid = r.astype(BF16)
    lo = (r - mid.astype(F32)).astype(BF16)
    return hi, mid, lo


def _dot_sel_lhs(sel, a):
    hi, mid, lo = _split3(a)
    return _dot(sel, hi) + _dot(sel, mid) + _dot(sel, lo)


def _dot_sel_rhs(a, sel):
    hi, mid, lo = _split3(a)
    return _dot(hi, sel) + _dot(mid, sel) + _dot(lo, sel)


def _silu(x):
    return x * jax.nn.sigmoid(x)


def _softplus(x):
    return jnp.maximum(x, 0.0) + jnp.log1p(jnp.exp(-jnp.abs(x)))


def _rope_cols(t, ra, rb, rc):
    outs = []
    for j in range(t.shape[1] // LANES):
        c = t[:, j * LANES:(j + 1) * LANES]
        outs.append(c * ra + pltpu.roll(c, LANES - ROT_DIM // 2, 1) * rb + pltpu.roll(c, ROT_DIM // 2, 1) * rc)
    return outs


def _inproj_body(x_ref, wq_ref, wk_ref, wv_ref, wvt_ref, wz_ref, wxbc_ref, wdt_ref, ra_ref, rb_ref, rc_ref,
                 q_ref, kf_ref, vf_ref, z_ref, xbc_ref, dt_ref, *maybe_bf16, attn_copies):
    xb = x_ref[...].astype(BF16)
    ra = ra_ref[...]
    rb = rb_ref[...]
    rc = rc_ref[...]
    q = _dot(xb, wq_ref[...])
    for j, c in enumerate(_rope_cols(q, ra, rb, rc)):
        q_ref[:, j * LANES:(j + 1) * LANES] = (c * Q_SCALE).astype(BF16)
    k = _dot(xb, wk_ref[...])
    for j, c in enumerate(_rope_cols(k, ra, rb, rc)):
        kf_ref[:, j * LANES:(j + 1) * LANES] = c
        if attn_copies:
            maybe_bf16[0][:, j * LANES:(j + 1) * LANES] = c.astype(BF16)
    vf_ref[...] = _dot(xb, wv_ref[...])
    if attn_copies:
        maybe_bf16[1][0, 0] = _dot_nt(wvt_ref[...], xb).astype(BF16)
    z_ref[...] = _dot(xb, wz_ref[...])
    xbc_ref[...] = _dot(xb, wxbc_ref[...])
    dt_ref[...] = _dot(xb, wdt_ref[...])


def _inproj(x2d, w, tabs, *, tm, nbatch, attn_copies):
    T = x2d.shape[0]
    nt = T // tm
    ntab = tabs[0].shape[0] // tm
    per_b = nt // nbatch
    full = lambda shape: pl.BlockSpec(shape, lambda i: (0,) * len(shape))
    row = lambda width: pl.BlockSpec((tm, width), lambda i: (i, 0))
    tab = pl.BlockSpec((tm, LANES), lambda i: (i % ntab, 0))
    in_specs = [row(D_MODEL), full((D_MODEL, 1024)), full((D_MODEL, 1024)), full((D_MODEL, 1024)),
                full((1024, D_MODEL)), full((D_MODEL, 1024)), full((D_MODEL, CONV_DIM)), full((D_MODEL, LANES)),
                tab, tab, tab]
    out_shape = [jax.ShapeDtypeStruct((T, 1024), BF16), jax.ShapeDtypeStruct((T, 1024), F32),
                 jax.ShapeDtypeStruct((T, 1024), F32), jax.ShapeDtypeStruct((T, 1024), F32),
                 jax.ShapeDtypeStruct((T, CONV_DIM), F32), jax.ShapeDtypeStruct((T, LANES), F32)]
    out_specs = [row(1024), row(1024), row(1024), row(1024), row(CONV_DIM), row(LANES)]
    if attn_copies:
        out_shape += [jax.ShapeDtypeStruct((T, 1024), BF16),
                      jax.ShapeDtypeStruct((nbatch, per_b, 1024, tm), BF16)]
        out_specs += [row(1024), pl.BlockSpec((1, 1, 1024, tm), lambda i: (i // per_b, i % per_b, 0, 0))]
    return pl.pallas_call(
        functools.partial(_inproj_body, attn_copies=attn_copies),
        grid=(nt,), in_specs=in_specs, out_specs=out_specs, out_shape=out_shape,
        compiler_params=_cparams(("parallel",)), name="inproj",
    )(x2d, w["wq"], w["wk"], w["wv"], w["wvt"], w["wz"], w["wxbc"], w["wdt"], *tabs)


def _ssd_chunk_math(xpad_ref, dt_raw, lmat, mtot, e64, e128, convw, convb, dtb, alog):
    acc = convb
    for j in range(CONV_W):
        acc = acc + xpad_ref[pl.ds(8 - (CONV_W - 1) + j, CHUNK), :] * convw[j:j + 1, :]
    act = _silu(acc)
    xs = act[:, :SSD_INNER]
    lane = lax.broadcasted_iota(jnp.int32, (1, LANES), 1)
    a_row = jnp.where(lane < SSD_HEADS, -jnp.exp(alog), 0.0)
    dtv = _softplus(dt_raw + dtb)
    dA = dtv * a_row
    cs = _dot_sel_lhs(lmat, dA)
    cst = _dot_sel_lhs(mtot, dA)
    csT = cs.T
    dt_e = _dot_sel_rhs(dtv, e64)
    cs_e = _dot_sel_rhs(cs, e64)
    cst_e = _dot_sel_rhs(cst, e64)
    cs_b = _dot_sel_rhs(cs, e128)
    X = xs * dt_e
    Xd = X * jnp.exp(cst_e - cs_e)
    ecs_e = jnp.exp(cs_e)
    lbool = lmat.astype(F32) > 0.5
    lane2 = lax.broadcasted_iota(jnp.int32, (CHUNK, LANES), 1)
    ydiag = []
    bgs, cgs = [], []
    for g in range(SSD_GROUPS):
        bg = act[:, SSD_INNER + g * D_STATE:SSD_INNER + (g + 1) * D_STATE].astype(BF16)
        cg = act[:, SSD_INNER + (SSD_GROUPS + g) * D_STATE:SSD_INNER + (SSD_GROUPS + g + 1) * D_STATE].astype(BF16)
        bgs.append(bg)
        cgs.append(cg)
        cb = _dot_nt(cg, bg)
        for jj in range(SSD_HEADS // SSD_GROUPS // 2):
            h0 = g * (SSD_HEADS // SSD_GROUPS) + 2 * jj
            xp = X[:, h0 * SSD_HD:h0 * SSD_HD + LANES].astype(BF16)
            yp = []
            for h in (h0, h0 + 1):
                seg = cs_b[:, h * LANES:(h + 1) * LANES] - csT[h:h + 1, :]
                lm = jnp.exp(jnp.where(lbool, seg, NEG))
                yp.append(_dot((cb * lm).astype(BF16), xp))
            ydiag.append(jnp.where(lane2 < SSD_HD, yp[0], yp[1]))
    return dict(act=act, xs=xs, Xd=Xd, ecs_e=ecs_e, csT=csT, ydiag=ydiag, bgs=bgs, cgs=cgs)


def _ssd_finish(y_cols, xs, z, dskip, normw, out_ref):
    gz = []
    for j in range(SSD_INNER // LANES):
        sl = slice(j * LANES, (j + 1) * LANES)
        y = y_cols[j] + dskip[:, sl] * xs[:, sl]
        gz.append(y * _silu(z[:, sl]))
    per_g = SSD_INNER // SSD_GROUPS // LANES
    for g in range(SSD_GROUPS):
        blk = gz[g * per_g:(g + 1) * per_g]
        ss = sum(jnp.sum(b * b, axis=1, keepdims=True) for b in blk)
        inv = lax.rsqrt(ss / (SSD_INNER // SSD_GROUPS) + EPS)
        for j, b in enumerate(blk):
            sl = slice((g * per_g + j) * LANES, (g * per_g + j + 1) * LANES)
            out_ref[:, sl] = (b * inv * normw[:, sl]).astype(out_ref.dtype)


def _ssd_prompt_body(xbc_ref, dt_ref, z_ref, lmat_ref, mtot_ref, e64_ref, e128_ref, convw_ref, convb_ref,
                     dtb_ref, alog_ref, dskip_ref, normw_ref, out_ref, state_ref, xpad_ref):
    c = pl.program_id(1)

    @pl.when(c == 0)
    def _():
        xpad_ref[pl.ds(0, 8), :] = jnp.zeros((8, CONV_DIM), F32)
        state_ref[...] = jnp.zeros(state_ref.shape, F32)

    xpad_ref[pl.ds(8, CHUNK), :] = xbc_ref[...]
    m = _ssd_chunk_math(xpad_ref, dt_ref[...], lmat_ref[...], mtot_ref[...], e64_ref[...], e128_ref[...],
                        convw_ref[...], convb_ref[...], dtb_ref[...], alog_ref[...])
    xpad_ref[pl.ds(8 - (CONV_W - 1), CONV_W - 1), :] = xpad_ref[pl.ds(8 + CHUNK - (CONV_W - 1), CONV_W - 1), :]

    csT = m["csT"]
    cd = jnp.exp(jnp.broadcast_to(csT[0:SSD_HEADS, CHUNK - 1:CHUNK], (SSD_HEADS, LANES)))
    hpg = SSD_HEADS // SSD_GROUPS
    gw = hpg * SSD_HD
    y_cols = []
    for g in range(SSD_GROUPS):
        sg = state_ref[0, pl.ds(g * gw, gw), :]
        yoff = _dot_nt(m["cgs"][g], sg.astype(BF16))
        for jj in range(gw // LANES):
            col = g * (gw // LANES) + jj
            y_cols.append(m["ydiag"][col] + yoff[:, jj * LANES:(jj + 1) * LANES]
                          * m["ecs_e"][:, col * LANES:(col + 1) * LANES])
        xdg_t = m["Xd"][:, g * gw:(g + 1) * gw].T.astype(BF16)
        contrib = _dot(xdg_t, m["bgs"][g])
        for hh in range(hpg):
            h = g * hpg + hh
            rows = pl.ds(h * SSD_HD, SSD_HD)
            state_ref[0, rows, :] = (state_ref[0, rows, :] * cd[h:h + 1, :]
                                     + contrib[hh * SSD_HD:(hh + 1) * SSD_HD, :])
    _ssd_finish(y_cols, m["xs"], z_ref[...], dskip_ref[...], normw_ref[...], out_ref)


def _ssd_consts(p):
    return [p["e64"], p["e128"], p["conv_w"], p["conv_b"], p["dt_bias"], p["a_log"], p["d_skip"], p["norm_w"]]


def _const_specs(nd_grid):
    z = (0, 0)
    if nd_grid == 2:
        f = lambda shape: pl.BlockSpec(shape, lambda b, c: z)
    else:
        f = lambda shape: pl.BlockSpec(shape, lambda i: z)
    return [f((LANES, SSD_INNER)), f((LANES, SSD_HEADS * LANES)), f((CONV_W, CONV_DIM)), f((1, CONV_DIM)),
            f((1, LANES)), f((1, LANES)), f((1, SSD_INNER)), f((1, SSD_INNER))]


def _ssd_prompt(xbc, dt, z, p, nbatch):
    T = xbc.shape[0]
    nc = T // nbatch // CHUNK
    row = lambda width: pl.BlockSpec((CHUNK, width), lambda b, c: (b * nc + c, 0))
    sq = pl.BlockSpec((CHUNK, CHUNK), lambda b, c: (0, 0))
    tri = jnp.tril(jnp.ones((CHUNK, CHUNK), F32)).astype(BF16)
    ones = jnp.ones((CHUNK, CHUNK), BF16)
    return pl.pallas_call(
        _ssd_prompt_body,
        grid=(nbatch, nc),
        in_specs=[row(CONV_DIM), row(LANES), row(SSD_INNER), sq, sq] + _const_specs(2),
        out_specs=[row(SSD_INNER), pl.BlockSpec((1, SSD_HEADS * SSD_HD, D_STATE), lambda b, c: (b, 0, 0))],
        out_shape=[jax.ShapeDtypeStruct((T, SSD_INNER), BF16),
                   jax.ShapeDtypeStruct((nbatch, SSD_HEADS * SSD_HD, D_STATE), F32)],
        scratch_shapes=[pltpu.VMEM((8 + CHUNK, CONV_DIM), F32)],
        compiler_params=_cparams(("parallel", "arbitrary")), name="ssd_prompt",
    )(xbc, dt, z, tri, ones, *_ssd_consts(p))


SROWS = 8
SB = CHUNK // SROWS


def _ssd_sample_body(xbc_ref, dt_ref, z_ref, s0_ref, lmat_ref, mtot_ref, e64_ref, e128_ref, convw_ref, convb_ref,
                     dtb_ref, alog_ref, dskip_ref, normw_ref, out_ref, s1_ref, xpad_ref):
    xpad_ref[pl.ds(0, 8), :] = jnp.zeros((8, CONV_DIM), F32)
    xpad_ref[pl.ds(8, CHUNK), :] = xbc_ref[...]
    m = _ssd_chunk_math(xpad_ref, dt_ref[...], lmat_ref[...], mtot_ref[...], e64_ref[...], e128_ref[...],
                        convw_ref[...], convb_ref[...], dtb_ref[...], alog_ref[...])
    csT = m["csT"]
    hpg = SSD_HEADS // SSD_GROUPS
    gw = hpg * SSD_HD
    rowi = lax.broadcasted_iota(jnp.int32, (CHUNK, LANES), 0)
    last_tok = CONV_W - 1 + DEC_Q - 1
    xdT = [m["Xd"][:, g * gw:(g + 1) * gw].T.astype(BF16) for g in range(SSD_GROUPS)]
    yoff = [[jnp.zeros((CHUNK, LANES), F32) for _ in range(gw // LANES)] for _ in range(SSD_GROUPS)]
    for b in range(SB):
        lo = b * SROWS + (CONV_W - 1)
        tok = (rowi >= lo) & (rowi <= b * SROWS + last_tok)
        col = b * SROWS + last_tok
        cd = jnp.exp(jnp.broadcast_to(csT[0:SSD_HEADS, col:col + 1], (SSD_HEADS, LANES)))
        for g in range(SSD_GROUPS):
            sg = s0_ref[b, pl.ds(g * gw, gw), :]
            res = _dot_nt(m["cgs"][g], sg.astype(BF16))
            for jj in range(gw // LANES):
                yoff[g][jj] = jnp.where(tok, res[:, jj * LANES:(jj + 1) * LANES], yoff[g][jj])
            bmask = jnp.where(tok, m["bgs"][g], jnp.zeros_like(m["bgs"][g]))
            contrib = _dot(xdT[g], bmask)
            for hh in range(hpg):
                h = g * hpg + hh
                rows = pl.ds(h * SSD_HD, SSD_HD)
                s1_ref[b, rows, :] = (s0_ref[b, rows, :] * cd[h:h + 1, :]
                                      + contrib[hh * SSD_HD:(hh + 1) * SSD_HD, :])
    y_cols = []
    for g in range(SSD_GROUPS):
        for jj in range(gw // LANES):
            col = g * (gw // LANES) + jj
            y_cols.append(m["ydiag"][col] + yoff[g][jj] * m["ecs_e"][:, col * LANES:(col + 1) * LANES])
    _ssd_finish(y_cols, m["xs"], z_ref[...], dskip_ref[...], normw_ref[...], out_ref)


def _ssd_sample(xbc8, dt8, z8, s0, p):
    T8 = xbc8.shape[0]
    nblk = T8 // CHUNK
    row = lambda width: pl.BlockSpec((CHUNK, width), lambda i: (i, 0))
    sq = pl.BlockSpec((CHUNK, CHUNK), lambda i: (0, 0))
    st = pl.BlockSpec((SB, SSD_HEADS * SSD_HD, D_STATE), lambda i: (i, 0, 0))
    r = jnp.arange(CHUNK)
    same = (r[:, None] // SROWS) == (r[None, :] // SROWS)
    is_tok = ((r % SROWS) >= CONV_W - 1) & ((r % SROWS) < SROWS - 1)
    lmat = (same & (r[None, :] <= r[:, None]) & is_tok[None, :] & is_tok[:, None]).astype(BF16)
    mtot = (same & is_tok[None, :]).astype(BF16)
    return pl.pallas_call(
        _ssd_sample_body,
        grid=(nblk,),
        in_specs=[row(CONV_DIM), row(LANES), row(SSD_INNER), st, sq, sq] + _const_specs(1),
        out_specs=[row(SSD_INNER), st],
        out_shape=[jax.ShapeDtypeStruct((T8, SSD_INNER), BF16), jax.ShapeDtypeStruct(s0.shape, F32)],
        scratch_shapes=[pltpu.VMEM((8 + CHUNK, CONV_DIM), F32)],
        compiler_params=_cparams(("parallel",)), name="ssd_sample",
    )(xbc8, dt8, z8, s0, lmat, mtot, *_ssd_consts(p))


def _lam_value(lq1, lk1, lq2, lk2, lam_init):
    s1 = jnp.sum(lq1[...] * lk1[...], axis=1, keepdims=True)
    s2 = jnp.sum(lq2[...] * lk2[...], axis=1, keepdims=True)
    return jnp.exp(s1) - jnp.exp(s2) + lam_init


def _flash_body(q_ref, k_ref, vt_ref, lq1, lk1, lq2, lk2, sw_ref, o_ref, sa_ref, sb_ref, bma_ref, bmb_ref,
                m_ref, l_ref, acc_ref, *, tq, tk, lam_init):
    i = pl.program_id(2)
    q = q_ref[...]
    lane = lax.broadcasted_iota(jnp.int32, (tq, LANES), 1)
    zero = jnp.zeros_like(q)
    qs = jnp.concatenate([jnp.where(lane < ATT_HD, q, zero), jnp.where(lane >= ATT_HD, q, zero)], axis=0)
    m_ref[...] = jnp.full(m_ref.shape, NEG, F32)
    l_ref[...] = jnp.zeros(l_ref.shape, F32)
    acc_ref[...] = jnp.zeros(acc_ref.shape, F32)

    def produce(blk, s_ref, bm_ref):
        kblk = k_ref[pl.ds(pl.multiple_of(blk * tk, tk), tk), :]
        s = _dot_nt(kblk, qs)
        s_ref[...] = s
        bm_ref[...] = jnp.max(s, axis=0, keepdims=True)

    def consume(blk, s_ref, bm_ref, off):
        s = s_ref[...]
        if off is None:
            bmax = bm_ref[...]
        else:
            kpos = lax.broadcasted_iota(jnp.int32, (tk, 2 * tq), 0) + off
            qpos = lax.broadcasted_iota(jnp.int32, (tk, 2 * tq), 1)
            qpos = jnp.where(qpos >= tq, qpos - tq, qpos)
            s = jnp.where(kpos <= qpos, s, NEG)
            bmax = jnp.max(s, axis=0, keepdims=True)
        m_old = m_ref[...]
        m_new = jnp.maximum(m_old, bmax)
        alpha = jnp.exp2(m_old - m_new)
        p = jnp.exp2(s - m_new)
        l_ref[...] = alpha * l_ref[...] + jnp.sum(p, axis=0, keepdims=True)
        m_ref[...] = m_new
        acc_ref[...] = alpha * acc_ref[...] + _dot(vt_ref[0, blk], p.astype(BF16))

    produce(0, sa_ref, bma_ref)

    def body(p, carry):
        produce(2 * p + 1, sb_ref, bmb_ref)
        consume(2 * p, sa_ref, bma_ref, None)
        produce(2 * p + 2, sa_ref, bma_ref)
        consume(2 * p + 1, sb_ref, bmb_ref, None)
        return carry

    lax.fori_loop(0, i, body, 0)
    produce(2 * i + 1, sb_ref, bmb_ref)
    consume(2 * i, sa_ref, bma_ref, 0)
    consume(2 * i + 1, sb_ref, bmb_ref, tk)

    lam = _lam_value(lq1, lk1, lq2, lk2, lam_init)
    acc = acc_ref[...]
    l = l_ref[...]
    o = acc[:, :tq] / l[:, :tq] - lam * (acc[:, tq:] / l[:, tq:])
    inv = lax.rsqrt(jnp.mean(o * o, axis=0, keepdims=True) + EPS)
    o_ref[...] = ((o * inv).T * (sw_ref[...] * (1.0 - lam_init))).astype(o_ref.dtype)


def _flash(q, kb, vt, lams, subln, *, lam_init):
    T = q.shape[0]
    nb, nk, _, tk = vt.shape
    tq = 2 * tk
    L = T // nb
    nq = L // tq
    vec = pl.BlockSpec((1, ATT_HD), lambda b, h, i: (0, 0))
    return pl.pallas_call(
        functools.partial(_flash_body, tq=tq, tk=tk, lam_init=lam_init),
        grid=(nb, ATT_HEADS, nq),
        in_specs=[pl.BlockSpec((tq, LANES), lambda b, h, i: (b * nq + i, h)),
                  pl.BlockSpec((L, LANES), lambda b, h, i: (b, h)),
                  pl.BlockSpec((1, nk, LANES, tk), lambda b, h, i: (b, 0, h, 0)),
                  vec, vec, vec, vec, pl.BlockSpec((1, LANES), lambda b, h, i: (0, 0))],
        out_specs=pl.BlockSpec((tq, LANES), lambda b, h, i: (b * nq + i, h)),
        out_shape=jax.ShapeDtypeStruct((T, ATT_WIDTH), BF16),
        scratch_shapes=[pltpu.VMEM((tk, 2 * tq), F32), pltpu.VMEM((tk, 2 * tq), F32),
                        pltpu.VMEM((1, 2 * tq), F32), pltpu.VMEM((1, 2 * tq), F32),
                        pltpu.VMEM((1, 2 * tq), F32), pltpu.VMEM((1, 2 * tq), F32),
                        pltpu.VMEM((LANES, 2 * tq), F32)],
        compiler_params=_cparams(("parallel", "parallel", "arbitrary")), name="flash_diff",
    )(q, kb, vt, *lams, subln)


PAGES_PER_STEP = 4
DEC_Q = 4


def _decode_body(pt_ref, qbd_ref, kn_ref, vx_ref, *refs, pps, page, lam_init):
    k_refs = refs[:pps]
    v_refs = refs[pps:2 * pps]
    lq1, lk1, lq2, lk2, sw_ref, o_ref, m_ref, l_ref, acc_ref = refs[2 * pps:]
    j = pl.program_id(1)
    nsteps = pl.num_programs(1)
    rows_per_head = 2 * DEC_Q

    @pl.when(j == 0)
    def _():
        m_ref[...] = jnp.full(m_ref.shape, NEG, F32)
        l_ref[...] = jnp.zeros(l_ref.shape, F32)
        acc_ref[...] = jnp.zeros(acc_ref.shape, F32)

    qbd = qbd_ref[0]
    s = jnp.concatenate(
        [_dot(qbd, k_refs[i][0].reshape(2 * ATT_HEADS * ATT_HD, page).astype(BF16)) for i in range(pps)], axis=1)
    m_old = m_ref[...]
    m_new = jnp.maximum(m_old, jnp.max(s, axis=1, keepdims=True))
    alpha = jnp.exp2(m_old - m_new)
    p = jnp.exp2(s - m_new)
    l_ref[...] = alpha * l_ref[...] + jnp.sum(p, axis=1, keepdims=True)
    m_ref[...] = m_new
    pb = p.astype(BF16)
    for h in range(ATT_HEADS):
        vh = jnp.concatenate([v_refs[i][0, pl.ds(h, page, stride=ATT_HEADS), :] for i in range(pps)],
                             axis=0).astype(BF16)
        rows = pl.ds(h * rows_per_head, rows_per_head)
        acc_ref[rows, :] = (alpha[h * rows_per_head:(h + 1) * rows_per_head, :] * acc_ref[rows, :]
                            + _dot(pb[h * rows_per_head:(h + 1) * rows_per_head, :], vh))

    @pl.when(j == nsteps - 1)
    def _():
        hist = CONV_W - 1
        sn = _dot_nt(qbd, kn_ref[0].astype(BF16))
        r = lax.broadcasted_iota(jnp.int32, sn.shape, 0)
        t = lax.broadcasted_iota(jnp.int32, sn.shape, 1) - hist
        sn = jnp.where((t >= 0) & (t < DEC_Q) & (t <= r % DEC_Q), sn, NEG)
        m_o = m_ref[...]
        m_n = jnp.maximum(m_o, jnp.max(sn, axis=1, keepdims=True))
        a = jnp.exp2(m_o - m_n)
        pn = jnp.exp2(sn - m_n)
        l = a * l_ref[...] + jnp.sum(pn, axis=1, keepdims=True)
        acc = a * acc_ref[...]
        for tt in range(DEC_Q):
            acc = acc + pn[:, hist + tt:hist + tt + 1] * vx_ref[0, tt]
        lam = _lam_value(lq1, lk1, lq2, lk2, lam_init)
        o1 = acc / l
        nrow = o1.shape[0]
        o = o1 - lam * pltpu.roll(o1, nrow - DEC_Q, 0)
        inv = lax.rsqrt(jnp.mean(o * o, axis=1, keepdims=True) + EPS)
        o_ref[0] = (o * inv * (sw_ref[...] * (1.0 - lam_init))).astype(o_ref.dtype)


def _decode_attn(qbd, knew, vexp, ckt, cv, page_table, lams, subln, *, lam_init):
    nb, npages = page_table.shape
    page = ckt.shape[3]
    pps = PAGES_PER_STEP
    nsteps = npages // pps
    nrow = 2 * ATT_HEADS * DEC_Q

    def kspec(i):
        return pl.BlockSpec((1, 2 * ATT_HEADS, ATT_HD, page), lambda b, j, pt: (pt[b, j * pps + i], 0, 0, 0))

    def vspec(i):
        return pl.BlockSpec((1, page * ATT_HEADS, LANES), lambda b, j, pt: (pt[b, j * pps + i], 0, 0))

    vec = pl.BlockSpec((1, ATT_HD), lambda b, j, pt: (0, 0))
    in_specs = ([pl.BlockSpec((1, nrow, 2 * ATT_HEADS * ATT_HD), lambda b, j, pt: (b, 0, 0)),
                 pl.BlockSpec((1, SROWS, 2 * ATT_HEADS * ATT_HD), lambda b, j, pt: (b, 0, 0)),
                 pl.BlockSpec((1, DEC_Q, nrow, LANES), lambda b, j, pt: (b, 0, 0, 0))]
                + [kspec(i) for i in range(pps)] + [vspec(i) for i in range(pps)]
                + [vec, vec, vec, vec, pl.BlockSpec((1, LANES), lambda b, j, pt: (0, 0))])
    grid_spec = pltpu.PrefetchScalarGridSpec(
        num_scalar_prefetch=1, grid=(nb, nsteps), in_specs=in_specs,
        out_specs=pl.BlockSpec((1, nrow, LANES), lambda b, j, pt: (b, 0, 0)),
        scratch_shapes=[pltpu.VMEM((nrow, 1), F32), pltpu.VMEM((nrow, 1), F32), pltpu.VMEM((nrow, LANES), F32)])
    return pl.pallas_call(
        functools.partial(_decode_body, pps=pps, page=page, lam_init=lam_init),
        grid_spec=grid_spec,
        out_shape=jax.ShapeDtypeStruct((nb, nrow, LANES), BF16),
        compiler_params=_cparams(("parallel", "arbitrary")), name="decode_attn",
    )(page_table, qbd, knew, vexp, *([ckt] * pps), *([cv] * pps), *lams, subln)


def _layer_norm(x, g, b):
    mu = jnp.mean(x, axis=1, keepdims=True)
    xc = x - mu
    var = jnp.mean(xc * xc, axis=1, keepdims=True)
    return xc * lax.rsqrt(var + EPS) * g + b


def _post_body(att_ref, ssd_ref, x_ref, pe_ref, woa_ref, wos_ref, wg_ref, wu_ref, wfo_ref, wpe_ref, wpg_ref,
               g1_ref, b1_ref, g2_ref, b2_ref, o_ref, *, alpha, nff):
    mix = _dot(att_ref[...], woa_ref[...]) + _dot(ssd_ref[...], wos_ref[...])
    h = _layer_norm(alpha * x_ref[...] + mix, g1_ref[...], b1_ref[...])
    hb = h.astype(BF16)
    ffn = None
    step = D_FF // nff
    for c in range(nff):
        sl = slice(c * step, (c + 1) * step)
        gate = _dot(hb, wg_ref[:, sl])
        up = _dot(hb, wu_ref[:, sl])
        part = _dot((_silu(gate) * up).astype(BF16), wfo_ref[sl, :])
        ffn = part if ffn is None else ffn + part
    pemb = _dot(pe_ref[...].astype(BF16), wpe_ref[...]) * jax.nn.sigmoid(_dot(hb, wpg_ref[...]))
    o_ref[...] = _layer_norm(alpha * h + ffn + pemb, g2_ref[...], b2_ref[...])


def _post(att, ssd, x2d, pe2d, w, *, tm, alpha):
    T = x2d.shape[0]
    row = lambda width: pl.BlockSpec((tm, width), lambda i: (i, 0))
    full = lambda shape: pl.BlockSpec(shape, lambda i: (0, 0), pipeline_mode=pl.Buffered(1))
    return pl.pallas_call(
        functools.partial(_post_body, alpha=alpha, nff=2),
        grid=(T // tm,),
        in_specs=[row(1024), row(1024), row(D_MODEL), row(P_DIM),
                  full((1024, D_MODEL)), full((1024, D_MODEL)), full((D_MODEL, D_FF)), full((D_MODEL, D_FF)),
                  full((D_FF, D_MODEL)), full((P_DIM, D_MODEL)), full((D_MODEL, D_MODEL)),
                  full((1, D_MODEL)), full((1, D_MODEL)), full((1, D_MODEL)), full((1, D_MODEL))],
        out_specs=row(D_MODEL),
        out_shape=jax.ShapeDtypeStruct((T, D_MODEL), F32),
        compiler_params=_cparams(("parallel",)), name="post",
    )(att, ssd, x2d, pe2d, w["woa"], w["wos"], w["wg"], w["wu"], w["wfo"], w["wpe"], w["wpg"],
      w["g1"], w["b1"], w["g2"], w["b2"])


def _rope_tables(pos):
    half = ROT_DIM // 2
    inv = ROPE_THETA ** (-jnp.arange(half, dtype=F32) * 2.0 / ROT_DIM)
    ang = pos.astype(F32)[:, None] * inv[None, :]
    cos, sin = jnp.cos(ang), jnp.sin(ang)
    n = pos.shape[0]
    pad = jnp.zeros((n, ATT_HD - ROT_DIM), F32)
    ra = jnp.concatenate([cos, cos, jnp.ones((n, ATT_HD - ROT_DIM), F32)], 1)
    rb = jnp.concatenate([-sin, jnp.zeros((n, half), F32), pad], 1)
    rc = jnp.concatenate([jnp.zeros((n, half), F32), sin, pad], 1)
    rep = LANES // ATT_HD
    return tuple(jnp.tile(t, (1, rep)) for t in (ra, rb, rc))


def _layer(depth, lam_init, x_prompt, x_sample, p_prompt, p_sample, cache_k, cache_v, page_table, state_ssm,
           state_conv, prm):
    (w_in, conv_w, conv_b, dt_bias, a_log, d_skip, ssd_norm_w, lq1, lk1, lq2, lk2, subln_w,
     w_out, ln1_g, ln1_b, w_ffn_in, w_ffn_out, w_pe, w_pg, ln2_g, ln2_b) = prm
    bp, lp, _ = x_prompt.shape
    bs, ls, _ = x_sample.shape
    npages = page_table.shape[1]
    page = cache_k.shape[1]
    past_len = npages * page
    alpha = (2 * depth) ** 0.25

    wb = w_in.astype(BF16)
    wq = wb[:, 0:1024]
    wk = wb[:, 1024:2048]
    wv = wb[:, 2048:3072]
    wz = wb[:, 3072:4096]
    wxbc = wb[:, 4096:4096 + CONV_DIM]
    wdt = jnp.pad(wb[:, 4096 + CONV_DIM:], ((0, 0), (0, LANES - SSD_HEADS)))
    w1 = dict(wq=wq, wk=wk, wv=wv, wvt=wv.T, wz=wz, wxbc=wxbc, wdt=wdt)

    hrep = jnp.arange(SSD_HEADS * SSD_HD) // SSD_HD
    e64 = (jnp.arange(LANES)[:, None] == hrep[None, :]).astype(BF16)
    e128 = (jnp.arange(LANES)[:, None] == (jnp.arange(SSD_HEADS * LANES) // LANES)[None, :]).astype(BF16)
    padl = lambda v: jnp.pad(v.astype(F32), (0, LANES - v.shape[0]))[None, :]
    pssd = dict(e64=e64, e128=e128, conv_w=conv_w.astype(F32), conv_b=conv_b.astype(F32)[None, :],
                dt_bias=padl(dt_bias), a_log=padl(a_log), d_skip=jnp.repeat(d_skip.astype(F32), SSD_HD)[None, :],
                norm_w=ssd_norm_w.astype(F32)[None, :])
    lams = [v.astype(F32)[None, :] for v in (lq1, lk1, lq2, lk2)]
    subln = subln_w.astype(F32)[None, :]
    wob = w_out.astype(BF16)
    wfi = w_ffn_in.astype(BF16)
    w4 = dict(woa=wob[:ATT_WIDTH], wos=wob[ATT_WIDTH:], wg=wfi[:, :D_FF], wu=wfi[:, D_FF:],
              wfo=w_ffn_out.astype(BF16), wpe=w_pe.astype(BF16), wpg=w_pg.astype(BF16),
              g1=ln1_g.astype(F32)[None, :], b1=ln1_b.astype(F32)[None, :],
              g2=ln2_g.astype(F32)[None, :], b2=ln2_b.astype(F32)[None, :])

    tm = 256
    xp2 = x_prompt.reshape(bp * lp, D_MODEL)
    q, kf, vf, z, xbc, dt, kb, vt = _inproj(xp2, w1, _rope_tables(jnp.arange(lp)), tm=tm, nbatch=bp,
                                            attn_copies=True)
    ssd_p, ssm_p = _ssd_prompt(xbc, dt, z, pssd, bp)
    att_p = _flash(q, kb, vt, lams, subln, lam_init=lam_init)
    y_p = _post(att_p, ssd_p, xp2, p_prompt.reshape(bp * lp, P_DIM), w4, tm=tm, alpha=alpha)
    conv_p = xbc.reshape(bp, lp, CONV_DIM)[:, lp - (CONV_W - 1):, :]

    hist = CONV_W - 1
    x8 = jnp.zeros((bs, SROWS, D_MODEL), F32).at[:, hist:hist + ls].set(x_sample)
    pos8 = jnp.tile(jnp.clip(jnp.arange(SROWS) - hist, 0, ls - 1) + past_len, bs)
    qs, kfs, vfs, zs, xbcs, dts = _inproj(x8.reshape(bs * SROWS, D_MODEL), w1, _rope_tables(pos8),
                                          tm=256, nbatch=1, attn_copies=False)
    xbc8 = xbcs.reshape(bs, SROWS, CONV_DIM).at[:, :hist].set(state_conv.astype(F32))
    ssd_s8, ssm_s = _ssd_sample(xbc8.reshape(bs * SROWS, CONV_DIM), dts, zs,
                                state_ssm.reshape(bs, SSD_HEADS * SSD_HD, D_STATE), pssd)
    tok = lambda a, w: a.reshape(bs, SROWS, w)[:, hist:hist + ls]
    ncomp = 2 * ATT_HEADS
    q4 = tok(qs, 1024).reshape(bs, ls, ncomp, ATT_HD)
    qcq = jnp.transpose(q4, (0, 2, 1, 3)).reshape(bs, ncomp * ls, ATT_HD)
    slot = (jnp.arange(ncomp * ls)[:, None] // ls) == (jnp.arange(ncomp * ATT_HD)[None, :] // ATT_HD)
    qbd = jnp.where(slot[None], jnp.tile(qcq, (1, 1, ncomp)), jnp.zeros((), BF16))
    k_s = tok(kfs, 1024)
    v_s = tok(vfs, 1024)
    vexp = jnp.repeat(v_s.reshape(bs, ls, ATT_HEADS, 2 * ATT_HD), 2 * ls, axis=2)
    att_s = _decode_attn(qbd, kfs.reshape(bs, SROWS, 1024), vexp,
                         jnp.transpose(cache_k, (0, 2, 3, 1)),
                         cache_v.reshape(cache_v.shape[0], page * ATT_HEADS, 2 * ATT_HD),
                         page_table, lams, subln, lam_init=lam_init)
    att_s = att_s.reshape(bs, ATT_HEADS, 2, ls, LANES)[:, :, 0]
    att_s = jnp.transpose(att_s, (0, 2, 1, 3)).reshape(bs * ls, ATT_WIDTH)
    ssd_s = tok(ssd_s8, SSD_INNER).reshape(bs * ls, SSD_INNER)
    y_s = _post(att_s, ssd_s, x_sample.reshape(bs * ls, D_MODEL), p_sample.reshape(bs * ls, P_DIM), w4,
                tm=min(256, bs * ls), alpha=alpha)
    conv_s = tok(xbcs, CONV_DIM)[:, ls - hist:]

    return (y_p.reshape(bp, lp, D_MODEL), y_s.reshape(bs, ls, D_MODEL),
            kf.reshape(bp, lp, 2 * ATT_HEADS, ATT_HD), vf.reshape(bp, lp, ATT_HEADS, 2 * ATT_HD),
            ssm_p.reshape(bp, SSD_HEADS, SSD_HD, D_STATE), conv_p,
            k_s.reshape(bs, ls, 2 * ATT_HEADS, ATT_HD), v_s.reshape(bs, ls, ATT_HEADS, 2 * ATT_HD),
            ssm_s.reshape(bs, SSD_HEADS, SSD_HD, D_STATE), conv_s)


def kernel(x_prompt, x_sample, p_prompt, p_sample, cache_k, cache_v, page_table, state_ssm, state_conv, w_in, conv_w, conv_b, dt_bias, a_log, d_skip, ssd_norm_w, lambda_q1, lambda_k1, lambda_q2, lambda_k2, subln_w, w_out, ln1_g, ln1_b, w_ffn_in, w_ffn_out, w_pe, w_pg, ln2_g, ln2_b):
    depth = w_in.shape[0]
    assert depth == 1, "single-layer trunk"
    prm = tuple(a[0] for a in (w_in, conv_w, conv_b, dt_bias, a_log, d_skip, ssd_norm_w, lambda_q1, lambda_k1,
                               lambda_q2, lambda_k2, subln_w, w_out, ln1_g, ln1_b, w_ffn_in, w_ffn_out, w_pe,
                               w_pg, ln2_g, ln2_b))
    lam_init = 0.8 - 0.6 * math.exp(-0.3 * 0)
    outs = _layer(depth, lam_init, x_prompt, x_sample, p_prompt[0], p_sample[0], cache_k[0], cache_v[0], page_table,
                  state_ssm[0], state_conv[0], prm)
    y_p, y_s = outs[0], outs[1]
    return (y_p, y_s) + tuple(o[None] for o in outs[2:])
```

```python
import functools
import math

import jax
import jax.numpy as jnp
from jax import lax
from jax.experimental import pallas as pl
from jax.experimental.pallas import tpu as pltpu

F32 = jnp.float32
BF16 = jnp.bfloat16

D_MODEL = 1024
ATT_HD = 64
ATT_HEADS = 8
ATT_WIDTH = 1024
ROT_DIM = 16
ROPE_THETA = 500000.0
SSD_HD = 64
SSD_INNER = 1024
SSD_HEADS = 16
SSD_GROUPS = 2
D_STATE = 128
CONV_W = 4
CONV_DIM = 1536
CHUNK = 128
D_FF = 2816
P_DIM = 256
EPS = 1e-5
LANES = 128
NEG = -1e30
Q_SCALE = (ATT_HD ** -0.5) * math.log2(math.e)

VMEM_LIMIT = 56 * 1024 * 1024


def _cparams(sem):
    return pltpu.CompilerParams(dimension_semantics=sem, vmem_limit_bytes=VMEM_LIMIT)


def _dot(a, b):
    return jnp.dot(a, b, preferred_element_type=F32)


def _dot_nt(a, b):
    return lax.dot_general(a, b, (((1,), (1,)), ((), ())), preferred_element_type=F32)


def _split3(a):
    hi = a.astype(BF16)
    r = a - hi.astype(F32)
    mid = r.astype(BF16)
    lo = (r - mid.astype(F32)).astype(BF16)
    return hi, mid, lo


def _dot_sel_lhs(sel3, a):
    hi, mid, lo = _split3(a)
    return _dot(sel3, jnp.concatenate([hi, mid, lo], axis=0))


def _dot_sel_rhs(a, sel3):
    hi, mid, lo = _split3(a)
    return _dot(jnp.concatenate([hi, mid, lo], axis=1), sel3)


def _silu(x):
    h = 0.5 * x
    return h + h * jnp.tanh(h)


def _softplus(x):
    return jnp.maximum(x, 0.0) + jnp.log1p(jnp.exp(-jnp.abs(x)))


def _rope_cols(t, ra, rb, rc):
    outs = []
    for j in range(t.shape[1] // LANES):
        c = t[:, j * LANES:(j + 1) * LANES]
        outs.append(c * ra + pltpu.roll(c, LANES - ROT_DIM // 2, 1) * rb + pltpu.roll(c, ROT_DIM // 2, 1) * rc)
    return outs


def _inproj_body(x_ref, wq_ref, wk_ref, wv_ref, wkt_ref, wvt_ref, wz_ref, wxbc_ref, wdt_ref, ra_ref, rb_ref, rc_ref,
                 ct_ref, st_ref, q_ref, kf_ref, vf_ref, z_ref, xbc_ref, dt_ref, *maybe_bf16, attn_copies):
    xb = x_ref[...].astype(BF16)
    ra = ra_ref[...]
    rb = rb_ref[...]
    rc = rc_ref[...]
    q = _dot(xb, wq_ref[...])
    for j, c in enumerate(_rope_cols(q, ra, rb, rc)):
        q_ref[:, j * LANES:(j + 1) * LANES] = (c * Q_SCALE).astype(BF16)
    k = _dot(xb, wk_ref[...])
    for j, c in enumerate(_rope_cols(k, ra, rb, rc)):
        if attn_copies:
            maybe_bf16[0][:, j * LANES:(j + 1) * LANES] = c.astype(BF16)
        else:
            kf_ref[:, j * LANES:(j + 1) * LANES] = c
    vf_ref[...] = _dot(xb, wv_ref[...])
    if attn_copies:
        maybe_bf16[1][0, 0] = _dot_nt(wvt_ref[...], xb).astype(BF16)
        kt = _dot_nt(wkt_ref[...], xb)
        kf_ref[0] = kt
        cos = ct_ref[...]
        sin = st_ref[...]
        half = ROT_DIM // 2
        for c in range(2 * ATT_HEADS):
            x1 = kt[c * ATT_HD:c * ATT_HD + half, :]
            x2 = kt[c * ATT_HD + half:c * ATT_HD + ROT_DIM, :]
            kf_ref[0, c * ATT_HD:c * ATT_HD + half, :] = x1 * cos - x2 * sin
            kf_ref[0, c * ATT_HD + half:c * ATT_HD + ROT_DIM, :] = x2 * cos + x1 * sin
    z_ref[...] = _dot(xb, wz_ref[...])
    xbc_ref[...] = _dot(xb, wxbc_ref[...])
    dt_ref[...] = _dot(xb, wdt_ref[...])


def _inproj(x2d, w, tabs, *, tm, nbatch, attn_copies):
    T = x2d.shape[0]
    nt = T // tm
    ntab = tabs[0].shape[0] // tm
    per_b = nt // nbatch
    full = lambda shape: pl.BlockSpec(shape, lambda i: (0,) * len(shape))
    row = lambda width: pl.BlockSpec((tm, width), lambda i: (i, 0))
    tab = pl.BlockSpec((tm, LANES), lambda i: (i % ntab, 0))
    tabt = pl.BlockSpec((ROT_DIM // 2, tm), lambda i: (0, i % ntab))
    in_specs = [row(D_MODEL), full((D_MODEL, 1024)), full((D_MODEL, 1024)), full((D_MODEL, 1024)),
                full((1024, D_MODEL)), full((1024, D_MODEL)), full((D_MODEL, 1024)), full((D_MODEL, CONV_DIM)),
                full((D_MODEL, LANES)), tab, tab, tab, tabt, tabt]
    k_shape, k_spec = jax.ShapeDtypeStruct((T, 1024), F32), row(1024)
    if attn_copies:
        k_shape = jax.ShapeDtypeStruct((nbatch, 1024, T // nbatch), F32)
        k_spec = pl.BlockSpec((1, 1024, tm), lambda i: (i // per_b, 0, i % per_b))
    out_shape = [jax.ShapeDtypeStruct((T, 1024), BF16), k_shape,
                 jax.ShapeDtypeStruct((T, 1024), F32), jax.ShapeDtypeStruct((T, 1024), F32),
                 jax.ShapeDtypeStruct((T, CONV_DIM), F32), jax.ShapeDtypeStruct((T, LANES), F32)]
    out_specs = [row(1024), k_spec, row(1024), row(1024), row(CONV_DIM), row(LANES)]
    if attn_copies:
        out_shape += [jax.ShapeDtypeStruct((T, 1024), BF16),
                      jax.ShapeDtypeStruct((nbatch, per_b, 1024, tm), BF16)]
        out_specs += [row(1024), pl.BlockSpec((1, 1, 1024, tm), lambda i: (i // per_b, i % per_b, 0, 0))]
    return pl.pallas_call(
        functools.partial(_inproj_body, attn_copies=attn_copies),
        grid=(nt,), in_specs=in_specs, out_specs=out_specs, out_shape=out_shape,
        compiler_params=_cparams(("parallel",)), name="inproj",
    )(x2d, w["wq"], w["wk"], w["wv"], w["wkt"], w["wvt"], w["wz"], w["wxbc"], w["wdt"], *tabs)


def _ssd_chunk_math(xpad_ref, dt_raw, lmat, mtot, e64, e128, convw, convb, dtb, alog):
    acc = convb
    for j in range(CONV_W):
        acc = acc + xpad_ref[pl.ds(8 - (CONV_W - 1) + j, CHUNK), :] * convw[j:j + 1, :]
    act = _silu(acc)
    xs = act[:, :SSD_INNER]
    lane = lax.broadcasted_iota(jnp.int32, (1, LANES), 1)
    a_row = jnp.where(lane < SSD_HEADS, -jnp.exp(alog), 0.0)
    dtv = _softplus(dt_raw + dtb)
    dA = dtv * a_row
    cs = _dot_sel_lhs(lmat, dA)
    cst = _dot_sel_lhs(mtot, dA)
    csT = cs.T
    ex = _dot_sel_rhs(jnp.concatenate([dtv, cs, cst], axis=0), e64)
    dt_e = ex[0:CHUNK]
    cs_e = ex[CHUNK:2 * CHUNK]
    cst_e = ex[2 * CHUNK:3 * CHUNK]
    cs_b = _dot_sel_rhs(cs, e128)
    X = xs * dt_e
    Xd = X * jnp.exp(cst_e - cs_e)
    ecs_e = jnp.exp(cs_e)
    lbool = lmat[:, 0:CHUNK].astype(F32) > 0.5
    lane2 = lax.broadcasted_iota(jnp.int32, (CHUNK, LANES), 1)
    ydiag = []
    bgs, cgs = [], []
    for g in range(SSD_GROUPS):
        bg = act[:, SSD_INNER + g * D_STATE:SSD_INNER + (g + 1) * D_STATE].astype(BF16)
        cg = act[:, SSD_INNER + (SSD_GROUPS + g) * D_STATE:SSD_INNER + (SSD_GROUPS + g + 1) * D_STATE].astype(BF16)
        bgs.append(bg)
        cgs.append(cg)
        cb = _dot_nt(cg, bg)
        for jj in range(SSD_HEADS // SSD_GROUPS // 2):
            h0 = g * (SSD_HEADS // SSD_GROUPS) + 2 * jj
            xp = X[:, h0 * SSD_HD:h0 * SSD_HD + LANES].astype(BF16)
            yp = []
            for h in (h0, h0 + 1):
                seg = cs_b[:, h * LANES:(h + 1) * LANES] - csT[h:h + 1, :]
                lm = jnp.exp(jnp.where(lbool, seg, NEG))
                yp.append(_dot((cb * lm).astype(BF16), xp))
            ydiag.append(jnp.where(lane2 < SSD_HD, yp[0], yp[1]))
    return dict(act=act, xs=xs, Xd=Xd, ecs_e=ecs_e, csT=csT, ydiag=ydiag, bgs=bgs, cgs=cgs)


def _ssd_finish(y_cols, xs, z, dskip, normw, out_ref):
    gz = []
    for j in range(SSD_INNER // LANES):
        sl = slice(j * LANES, (j + 1) * LANES)
        y = y_cols[j] + dskip[:, sl] * xs[:, sl]
        gz.append(y * _silu(z[:, sl]))
    per_g = SSD_INNER // SSD_GROUPS // LANES
    for g in range(SSD_GROUPS):
        blk = gz[g * per_g:(g + 1) * per_g]
        ss = sum(jnp.sum(b * b, axis=1, keepdims=True) for b in blk)
        inv = lax.rsqrt(ss / (SSD_INNER // SSD_GROUPS) + EPS)
        for j, b in enumerate(blk):
            sl = slice((g * per_g + j) * LANES, (g * per_g + j + 1) * LANES)
            out_ref[:, sl] = (b * inv * normw[:, sl]).astype(out_ref.dtype)


def _ssd_prompt_body(xbc_ref, dt_ref, z_ref, lmat_ref, mtot_ref, e64_ref, e128_ref, convw_ref, convb_ref,
                     dtb_ref, alog_ref, dskip_ref, normw_ref, out_ref, state_ref, xpad_ref):
    c = pl.program_id(1)

    @pl.when(c == 0)
    def _():
        xpad_ref[pl.ds(0, 8), :] = jnp.zeros((8, CONV_DIM), F32)
        state_ref[...] = jnp.zeros(state_ref.shape, F32)

    xpad_ref[pl.ds(8, CHUNK), :] = xbc_ref[...]
    m = _ssd_chunk_math(xpad_ref, dt_ref[...], lmat_ref[...], mtot_ref[...], e64_ref[...], e128_ref[...],
                        convw_ref[...], convb_ref[...], dtb_ref[...], alog_ref[...])
    xpad_ref[pl.ds(8 - (CONV_W - 1), CONV_W - 1), :] = xpad_ref[pl.ds(8 + CHUNK - (CONV_W - 1), CONV_W - 1), :]

    csT = m["csT"]
    cd = jnp.exp(jnp.broadcast_to(csT[0:SSD_HEADS, CHUNK - 1:CHUNK], (SSD_HEADS, LANES)))
    hpg = SSD_HEADS // SSD_GROUPS
    gw = hpg * SSD_HD
    y_cols = []
    for g in range(SSD_GROUPS):
        sg = state_ref[0, pl.ds(g * gw, gw), :]
        yoff = _dot_nt(m["cgs"][g], sg.astype(BF16))
        for jj in range(gw // LANES):
            col = g * (gw // LANES) + jj
            y_cols.append(m["ydiag"][col] + yoff[:, jj * LANES:(jj + 1) * LANES]
                          * m["ecs_e"][:, col * LANES:(col + 1) * LANES])
        xdg_t = m["Xd"][:, g * gw:(g + 1) * gw].T.astype(BF16)
        contrib = _dot(xdg_t, m["bgs"][g])
        for hh in range(hpg):
            h = g * hpg + hh
            rows = pl.ds(h * SSD_HD, SSD_HD)
            state_ref[0, rows, :] = (state_ref[0, rows, :] * cd[h:h + 1, :]
                                     + contrib[hh * SSD_HD:(hh + 1) * SSD_HD, :])
    _ssd_finish(y_cols, m["xs"], z_ref[...], dskip_ref[...], normw_ref[...], out_ref)


def _ssd_consts(p):
    return [p["e64"], p["e128"], p["conv_w"], p["conv_b"], p["dt_bias"], p["a_log"], p["d_skip"], p["norm_w"]]


def _const_specs(nd_grid):
    z = (0, 0)
    if nd_grid == 2:
        f = lambda shape: pl.BlockSpec(shape, lambda b, c: z)
    else:
        f = lambda shape: pl.BlockSpec(shape, lambda i: z)
    return [f((3 * LANES, SSD_INNER)), f((3 * LANES, SSD_HEADS * LANES)), f((CONV_W, CONV_DIM)), f((1, CONV_DIM)),
            f((1, LANES)), f((1, LANES)), f((1, SSD_INNER)), f((1, SSD_INNER))]


def _ssd_prompt(xbc, dt, z, p, nbatch):
    T = xbc.shape[0]
    nc = T // nbatch // CHUNK
    row = lambda width: pl.BlockSpec((CHUNK, width), lambda b, c: (b * nc + c, 0))
    sq = pl.BlockSpec((CHUNK, 3 * CHUNK), lambda b, c: (0, 0))
    tri = jnp.tile(jnp.tril(jnp.ones((CHUNK, CHUNK), F32)).astype(BF16), (1, 3))
    ones = jnp.ones((CHUNK, 3 * CHUNK), BF16)
    return pl.pallas_call(
        _ssd_prompt_body,
        grid=(nbatch, nc),
        in_specs=[row(CONV_DIM), row(LANES), row(SSD_INNER), sq, sq] + _const_specs(2),
        out_specs=[row(SSD_INNER), pl.BlockSpec((1, SSD_HEADS * SSD_HD, D_STATE), lambda b, c: (b, 0, 0))],
        out_shape=[jax.ShapeDtypeStruct((T, SSD_INNER), BF16),
                   jax.ShapeDtypeStruct((nbatch, SSD_HEADS * SSD_HD, D_STATE), F32)],
        scratch_shapes=[pltpu.VMEM((8 + CHUNK, CONV_DIM), F32)],
        compiler_params=_cparams(("parallel", "arbitrary")), name="ssd_prompt",
    )(xbc, dt, z, tri, ones, *_ssd_consts(p))


SROWS = 8
SB = CHUNK // SROWS


def _ssd_sample_body(xbc_ref, dt_ref, z_ref, s0_ref, lmat_ref, mtot_ref, e64_ref, e128_ref, convw_ref, convb_ref,
                     dtb_ref, alog_ref, dskip_ref, normw_ref, out_ref, s1_ref, xpad_ref):
    xpad_ref[pl.ds(0, 8), :] = jnp.zeros((8, CONV_DIM), F32)
    xpad_ref[pl.ds(8, CHUNK), :] = xbc_ref[...]
    m = _ssd_chunk_math(xpad_ref, dt_ref[...], lmat_ref[...], mtot_ref[...], e64_ref[...], e128_ref[...],
                        convw_ref[...], convb_ref[...], dtb_ref[...], alog_ref[...])
    csT = m["csT"]
    hpg = SSD_HEADS // SSD_GROUPS
    gw = hpg * SSD_HD
    rowi = lax.broadcasted_iota(jnp.int32, (CHUNK, LANES), 0)
    last_tok = CONV_W - 1 + DEC_Q - 1
    xdT = [m["Xd"][:, g * gw:(g + 1) * gw].T.astype(BF16) for g in range(SSD_GROUPS)]
    yoff = [[jnp.zeros((CHUNK, LANES), F32) for _ in range(gw // LANES)] for _ in range(SSD_GROUPS)]
    for b in range(SB):
        lo = b * SROWS + (CONV_W - 1)
        tok = (rowi >= lo) & (rowi <= b * SROWS + last_tok)
        col = b * SROWS + last_tok
        cd = jnp.exp(jnp.broadcast_to(csT[0:SSD_HEADS, col:col + 1], (SSD_HEADS, LANES)))
        for g in range(SSD_GROUPS):
            sg = s0_ref[b, pl.ds(g * gw, gw), :]
            res = _dot_nt(m["cgs"][g], sg.astype(BF16))
            for jj in range(gw // LANES):
                yoff[g][jj] = jnp.where(tok, res[:, jj * LANES:(jj + 1) * LANES], yoff[g][jj])
            bmask = jnp.where(tok, m["bgs"][g], jnp.zeros_like(m["bgs"][g]))
            contrib = _dot(xdT[g], bmask)
            for hh in range(hpg):
                h = g * hpg + hh
                rows = pl.ds(h * SSD_HD, SSD_HD)
                s1_ref[b, rows, :] = (s0_ref[b, rows, :] * cd[h:h + 1, :]
                                      + contrib[hh * SSD_HD:(hh + 1) * SSD_HD, :])
    y_cols = []
    for g in range(SSD_GROUPS):
        for jj in range(gw // LANES):
            col = g * (gw // LANES) + jj
            y_cols.append(m["ydiag"][col] + yoff[g][jj] * m["ecs_e"][:, col * LANES:(col + 1) * LANES])
    _ssd_finish(y_cols, m["xs"], z_ref[...], dskip_ref[...], normw_ref[...], out_ref)


def _ssd_sample(xbc8, dt8, z8, s0, p):
    T8 = xbc8.shape[0]
    nblk = T8 // CHUNK
    row = lambda width: pl.BlockSpec((CHUNK, width), lambda i: (i, 0))
    sq = pl.BlockSpec((CHUNK, 3 * CHUNK), lambda i: (0, 0))
    st = pl.BlockSpec((SB, SSD_HEADS * SSD_HD, D_STATE), lambda i: (i, 0, 0))
    r = jnp.arange(CHUNK)
    same = (r[:, None] // SROWS) == (r[None, :] // SROWS)
    is_tok = ((r % SROWS) >= CONV_W - 1) & ((r % SROWS) < SROWS - 1)
    lmat = jnp.tile((same & (r[None, :] <= r[:, None]) & is_tok[None, :] & is_tok[:, None]).astype(BF16), (1, 3))
    mtot = jnp.tile((same & is_tok[None, :]).astype(BF16), (1, 3))
    return pl.pallas_call(
        _ssd_sample_body,
        grid=(nblk,),
        in_specs=[row(CONV_DIM), row(LANES), row(SSD_INNER), st, sq, sq] + _const_specs(1),
        out_specs=[row(SSD_INNER), st],
        out_shape=[jax.ShapeDtypeStruct((T8, SSD_INNER), BF16), jax.ShapeDtypeStruct(s0.shape, F32)],
        scratch_shapes=[pltpu.VMEM((8 + CHUNK, CONV_DIM), F32)],
        compiler_params=_cparams(("parallel",)), name="ssd_sample",
    )(xbc8, dt8, z8, s0, lmat, mtot, *_ssd_consts(p))


def _lam_value(lq1, lk1, lq2, lk2, lam_init):
    s1 = jnp.sum(lq1[...] * lk1[...], axis=1, keepdims=True)
    s2 = jnp.sum(lq2[...] * lk2[...], axis=1, keepdims=True)
    return jnp.exp(s1) - jnp.exp(s2) + lam_init


def _flash_body(q_ref, k_ref, vt_ref, lq1, lk1, lq2, lk2, sw_ref, o_ref, sa_ref, sb_ref, bma_ref, bmb_ref,
                m_ref, acc_ref, *, tq, tk, lam_init):
    i = pl.program_id(2)
    qt = q_ref[...].astype(F32).T
    row = lax.broadcasted_iota(jnp.int32, (LANES, tq), 0)
    qst = jnp.concatenate([jnp.where(row < ATT_HD, qt, 0.0), jnp.where(row >= ATT_HD, qt, 0.0)],
                          axis=1).astype(BF16)
    ones = jnp.ones((ONES_ROWS, tk), BF16)
    m_ref[...] = jnp.full(m_ref.shape, NEG, F32)
    acc_ref[...] = jnp.zeros(acc_ref.shape, F32)

    def produce(blk, s_ref, bm_ref):
        kblk = k_ref[pl.ds(pl.multiple_of(blk * tk, tk), tk), :]
        s = _dot(kblk, qst)
        s_ref[...] = s
        bm_ref[...] = jnp.max(s, axis=0, keepdims=True)

    def consume(blk, s_ref, bm_ref, off):
        s = s_ref[...]
        if off is None:
            bmax = bm_ref[...]
        else:
            kpos = lax.broadcasted_iota(jnp.int32, (tk, 2 * tq), 0) + off
            qpos = lax.broadcasted_iota(jnp.int32, (tk, 2 * tq), 1)
            qpos = jnp.where(qpos >= tq, qpos - tq, qpos)
            s = jnp.where(kpos <= qpos, s, NEG)
            bmax = jnp.max(s, axis=0, keepdims=True)
        m_old = m_ref[...]
        m_new = jnp.maximum(m_old, bmax)
        alpha = jnp.exp2(m_old - m_new)
        p = jnp.exp2((s - m_new).astype(BF16))
        m_ref[...] = m_new
        vt1 = jnp.concatenate([vt_ref[0, blk], ones], axis=0)
        acc_ref[...] = alpha * acc_ref[...] + _dot(vt1, p)

    produce(0, sa_ref, bma_ref)

    def body(p, carry):
        produce(2 * p + 1, sb_ref, bmb_ref)
        consume(2 * p, sa_ref, bma_ref, None)
        produce(2 * p + 2, sa_ref, bma_ref)
        consume(2 * p + 1, sb_ref, bmb_ref, None)
        return carry

    lax.fori_loop(0, i, body, 0)
    produce(2 * i + 1, sb_ref, bmb_ref)
    consume(2 * i, sa_ref, bma_ref, 0)
    consume(2 * i + 1, sb_ref, bmb_ref, tk)

    lam = _lam_value(lq1, lk1, lq2, lk2, lam_init)
    acc = acc_ref[0:LANES, :]
    l = acc_ref[LANES:LANES + 1, :]
    o = acc[:, :tq] / l[:, :tq] - lam * (acc[:, tq:] / l[:, tq:])
    inv = lax.rsqrt(jnp.mean(o * o, axis=0, keepdims=True) + EPS)
    o_ref[...] = ((o * inv).T * (sw_ref[...] * (1.0 - lam_init))).astype(o_ref.dtype)


def _flash(q, kb, vt, lams, subln, *, lam_init):
    T = q.shape[0]
    nb, nk, _, tk = vt.shape
    tq = 2 * tk
    L = T // nb
    nq = L // tq
    vec = pl.BlockSpec((1, ATT_HD), lambda b, h, i: (0, 0))
    return pl.pallas_call(
        functools.partial(_flash_body, tq=tq, tk=tk, lam_init=lam_init),
        grid=(nb, ATT_HEADS, nq),
        in_specs=[pl.BlockSpec((tq, LANES), lambda b, h, i: (b * nq + i, h)),
                  pl.BlockSpec((L, LANES), lambda b, h, i: (b, h)),
                  pl.BlockSpec((1, nk, LANES, tk), lambda b, h, i: (b, 0, h, 0)),
                  vec, vec, vec, vec, pl.BlockSpec((1, LANES), lambda b, h, i: (0, 0))],
        out_specs=pl.BlockSpec((tq, LANES), lambda b, h, i: (b * nq + i, h)),
        out_shape=jax.ShapeDtypeStruct((T, ATT_WIDTH), BF16),
        scratch_shapes=[pltpu.VMEM((tk, 2 * tq), F32), pltpu.VMEM((tk, 2 * tq), F32),
                        pltpu.VMEM((1, 2 * tq), F32), pltpu.VMEM((1, 2 * tq), F32),
                        pltpu.VMEM((1, 2 * tq), F32), pltpu.VMEM((LANES + ONES_ROWS, 2 * tq), F32)],
        compiler_params=_cparams(("parallel", "parallel", "arbitrary")), name="flash_diff",
    )(q, kb, vt, *lams, subln)


ONES_ROWS = 16
PAGES_PER_STEP = 8
DEC_Q = 4


def _decode_body(pt_ref, qbd_ref, kn_ref, vx_ref, *refs, pps, page, lam_init):
    k_refs = refs[:pps]
    v_refs = refs[pps:2 * pps]
    lq1, lk1, lq2, lk2, sw_ref, o_ref, m_ref, l_ref, acc_ref = refs[2 * pps:]
    j = pl.program_id(1)
    nsteps = pl.num_programs(1)
    rows_per_head = 2 * DEC_Q

    @pl.when(j == 0)
    def _():
        m_ref[...] = jnp.full(m_ref.shape, NEG, F32)
        l_ref[...] = jnp.zeros(l_ref.shape, F32)
        acc_ref[...] = jnp.zeros(acc_ref.shape, F32)

    qbd = qbd_ref[0]
    s = jnp.concatenate(
        [_dot(qbd, k_refs[i][0].reshape(2 * ATT_HEADS * ATT_HD, page).astype(BF16)) for i in range(pps)], axis=1)
    m_old = m_ref[...]
    m_new = jnp.maximum(m_old, jnp.max(s, axis=1, keepdims=True))
    alpha = jnp.exp2(m_old - m_new)
    p = jnp.exp2(s - m_new)
    l_ref[...] = alpha * l_ref[...] + jnp.sum(p, axis=1, keepdims=True)
    m_ref[...] = m_new
    pb = p.astype(BF16)
    for h in range(ATT_HEADS):
        vh = jnp.concatenate([v_refs[i][0, pl.ds(h, page, stride=ATT_HEADS), :] for i in range(pps)],
                             axis=0).astype(BF16)
        rows = pl.ds(h * rows_per_head, rows_per_head)
        acc_ref[rows, :] = (alpha[h * rows_per_head:(h + 1) * rows_per_head, :] * acc_ref[rows, :]
                            + _dot(pb[h * rows_per_head:(h + 1) * rows_per_head, :], vh))

    @pl.when(j == nsteps - 1)
    def _():
        hist = CONV_W - 1
        sn = _dot_nt(qbd, kn_ref[0].astype(BF16))
        r = lax.broadcasted_iota(jnp.int32, sn.shape, 0)
        t = lax.broadcasted_iota(jnp.int32, sn.shape, 1) - hist
        sn = jnp.where((t >= 0) & (t < DEC_Q) & (t <= r % DEC_Q), sn, NEG)
        m_o = m_ref[...]
        m_n = jnp.maximum(m_o, jnp.max(sn, axis=1, keepdims=True))
        a = jnp.exp2(m_o - m_n)
        pn = jnp.exp2(sn - m_n)
        l = a * l_ref[...] + jnp.sum(pn, axis=1, keepdims=True)
        acc = a * acc_ref[...]
        for tt in range(DEC_Q):
            acc = acc + pn[:, hist + tt:hist + tt + 1] * vx_ref[0, tt]
        lam = _lam_value(lq1, lk1, lq2, lk2, lam_init)
        o1 = acc / l
        nrow = o1.shape[0]
        o = o1 - lam * pltpu.roll(o1, nrow - DEC_Q, 0)
        inv = lax.rsqrt(jnp.mean(o * o, axis=1, keepdims=True) + EPS)
        o_ref[0] = (o * inv * (sw_ref[...] * (1.0 - lam_init))).astype(o_ref.dtype)


def _decode_attn(qbd, knew, vexp, ckt, cv, page_table, lams, subln, *, lam_init):
    nb, npages = page_table.shape
    page = ckt.shape[3]
    pps = PAGES_PER_STEP
    nsteps = npages // pps
    nrow = 2 * ATT_HEADS * DEC_Q

    def kspec(i):
        return pl.BlockSpec((1, 2 * ATT_HEADS, ATT_HD, page), lambda b, j, pt: (pt[b, j * pps + i], 0, 0, 0))

    def vspec(i):
        return pl.BlockSpec((1, page * ATT_HEADS, LANES), lambda b, j, pt: (pt[b, j * pps + i], 0, 0))

    vec = pl.BlockSpec((1, ATT_HD), lambda b, j, pt: (0, 0))
    in_specs = ([pl.BlockSpec((1, nrow, 2 * ATT_HEADS * ATT_HD), lambda b, j, pt: (b, 0, 0)),
                 pl.BlockSpec((1, SROWS, 2 * ATT_HEADS * ATT_HD), lambda b, j, pt: (b, 0, 0)),
                 pl.BlockSpec((1, DEC_Q, nrow, LANES), lambda b, j, pt: (b, 0, 0, 0))]
                + [kspec(i) for i in range(pps)] + [vspec(i) for i in range(pps)]
                + [vec, vec, vec, vec, pl.BlockSpec((1, LANES), lambda b, j, pt: (0, 0))])
    grid_spec = pltpu.PrefetchScalarGridSpec(
        num_scalar_prefetch=1, grid=(nb, nsteps), in_specs=in_specs,
        out_specs=pl.BlockSpec((1, nrow, LANES), lambda b, j, pt: (b, 0, 0)),
        scratch_shapes=[pltpu.VMEM((nrow, 1), F32), pltpu.VMEM((nrow, 1), F32), pltpu.VMEM((nrow, LANES), F32)])
    return pl.pallas_call(
        functools.partial(_decode_body, pps=pps, page=page, lam_init=lam_init),
        grid_spec=grid_spec,
        out_shape=jax.ShapeDtypeStruct((nb, nrow, LANES), BF16),
        compiler_params=_cparams(("parallel", "arbitrary")), name="decode_attn",
    )(page_table, qbd, knew, vexp, *([ckt] * pps), *([cv] * pps), *lams, subln)


def _layer_norm(x, g, b):
    mu = jnp.mean(x, axis=1, keepdims=True)
    xc = x - mu
    var = jnp.mean(xc * xc, axis=1, keepdims=True)
    return xc * lax.rsqrt(var + EPS) * g + b


def _post_body(att_ref, ssd_ref, x_ref, pe_ref, woa_ref, wos_ref, wg_ref, wu_ref, wfo_ref, wpe_ref, wpg_ref,
               g1_ref, b1_ref, g2_ref, b2_ref, o_ref, *, alpha, nff):
    mix = _dot(att_ref[...], woa_ref[...]) + _dot(ssd_ref[...], wos_ref[...])
    h = _layer_norm(alpha * x_ref[...] + mix, g1_ref[...], b1_ref[...])
    hb = h.astype(BF16)
    ffn = None
    step = D_FF // nff
    for c in range(nff):
        sl = slice(c * step, (c + 1) * step)
        gate = _dot(hb, wg_ref[:, sl])
        up = _dot(hb, wu_ref[:, sl])
        part = _dot((_silu(gate) * up).astype(BF16), wfo_ref[sl, :])
        ffn = part if ffn is None else ffn + part
    pemb = _dot(pe_ref[...].astype(BF16), wpe_ref[...]) * jax.nn.sigmoid(_dot(hb, wpg_ref[...]))
    o_ref[...] = _layer_norm(alpha * h + ffn + pemb, g2_ref[...], b2_ref[...])


def _post(att, ssd, x2d, pe2d, w, *, tm, alpha):
    T = x2d.shape[0]
    row = lambda width: pl.BlockSpec((tm, width), lambda i: (i, 0))
    full = lambda shape: pl.BlockSpec(shape, lambda i: (0, 0), pipeline_mode=pl.Buffered(1))
    return pl.pallas_call(
        functools.partial(_post_body, alpha=alpha, nff=2),
        grid=(T // tm,),
        in_specs=[row(1024), row(1024), row(D_MODEL), row(P_DIM),
                  full((1024, D_MODEL)), full((1024, D_MODEL)), full((D_MODEL, D_FF)), full((D_MODEL, D_FF)),
                  full((D_FF, D_MODEL)), full((P_DIM, D_MODEL)), full((D_MODEL, D_MODEL)),
                  full((1, D_MODEL)), full((1, D_MODEL)), full((1, D_MODEL)), full((1, D_MODEL))],
        out_specs=row(D_MODEL),
        out_shape=jax.ShapeDtypeStruct((T, D_MODEL), F32),
        compiler_params=_cparams(("parallel",)), name="post",
    )(att, ssd, x2d, pe2d, w["woa"], w["wos"], w["wg"], w["wu"], w["wfo"], w["wpe"], w["wpg"],
      w["g1"], w["b1"], w["g2"], w["b2"])


def _rope_tables(pos):
    half = ROT_DIM // 2
    inv = ROPE_THETA ** (-jnp.arange(half, dtype=F32) * 2.0 / ROT_DIM)
    ang = pos.astype(F32)[:, None] * inv[None, :]
    cos, sin = jnp.cos(ang), jnp.sin(ang)
    n = pos.shape[0]
    pad = jnp.zeros((n, ATT_HD - ROT_DIM), F32)
    ra = jnp.concatenate([cos, cos, jnp.ones((n, ATT_HD - ROT_DIM), F32)], 1)
    rb = jnp.concatenate([-sin, jnp.zeros((n, half), F32), pad], 1)
    rc = jnp.concatenate([jnp.zeros((n, half), F32), sin, pad], 1)
    rep = LANES // ATT_HD
    return tuple(jnp.tile(t, (1, rep)) for t in (ra, rb, rc)) + (cos.T, sin.T)


def _layer(depth, lam_init, x_prompt, x_sample, p_prompt, p_sample, cache_k, cache_v, page_table, state_ssm,
           state_conv, prm):
    (w_in, conv_w, conv_b, dt_bias, a_log, d_skip, ssd_norm_w, lq1, lk1, lq2, lk2, subln_w,
     w_out, ln1_g, ln1_b, w_ffn_in, w_ffn_out, w_pe, w_pg, ln2_g, ln2_b) = prm
    bp, lp, _ = x_prompt.shape
    bs, ls, _ = x_sample.shape
    npages = page_table.shape[1]
    page = cache_k.shape[1]
    past_len = npages * page
    alpha = (2 * depth) ** 0.25

    wb = w_in.astype(BF16)
    wq = wb[:, 0:1024]
    wk = wb[:, 1024:2048]
    wv = wb[:, 2048:3072]
    wz = wb[:, 3072:4096]
    wxbc = wb[:, 4096:4096 + CONV_DIM]
    wdt = jnp.pad(wb[:, 4096 + CONV_DIM:], ((0, 0), (0, LANES - SSD_HEADS)))
    w1 = dict(wq=wq, wk=wk, wv=wv, wkt=wk.T, wvt=wv.T, wz=wz, wxbc=wxbc, wdt=wdt)

    hrep = jnp.arange(SSD_HEADS * SSD_HD) // SSD_HD
    e64 = (jnp.arange(LANES)[:, None] == hrep[None, :]).astype(BF16)
    e128 = (jnp.arange(LANES)[:, None] == (jnp.arange(SSD_HEADS * LANES) // LANES)[None, :]).astype(BF16)
    padl = lambda v: jnp.pad(v.astype(F32), (0, LANES - v.shape[0]))[None, :]
    pssd = dict(e64=jnp.tile(e64, (3, 1)), e128=jnp.tile(e128, (3, 1)), conv_w=conv_w.astype(F32), conv_b=conv_b.astype(F32)[None, :],
                dt_bias=padl(dt_bias), a_log=padl(a_log), d_skip=jnp.repeat(d_skip.astype(F32), SSD_HD)[None, :],
                norm_w=ssd_norm_w.astype(F32)[None, :])
    lams = [v.astype(F32)[None, :] for v in (lq1, lk1, lq2, lk2)]
    subln = subln_w.astype(F32)[None, :]
    wob = w_out.astype(BF16)
    wfi = w_ffn_in.astype(BF16)
    w4 = dict(woa=wob[:ATT_WIDTH], wos=wob[ATT_WIDTH:], wg=wfi[:, :D_FF], wu=wfi[:, D_FF:],
              wfo=w_ffn_out.astype(BF16), wpe=w_pe.astype(BF16), wpg=w_pg.astype(BF16),
              g1=ln1_g.astype(F32)[None, :], b1=ln1_b.astype(F32)[None, :],
              g2=ln2_g.astype(F32)[None, :], b2=ln2_b.astype(F32)[None, :])

    tm = 256
    xp2 = x_prompt.reshape(bp * lp, D_MODEL)
    q, kt, vf, z, xbc, dt, kb, vt = _inproj(xp2, w1, _rope_tables(jnp.arange(lp)), tm=tm, nbatch=bp,
                                            attn_copies=True)
    k_p = jnp.transpose(kt.reshape(bp, 2 * ATT_HEADS, ATT_HD, lp), (0, 3, 1, 2))
    ssd_p, ssm_p = _ssd_prompt(xbc, dt, z, pssd, bp)
    att_p = _flash(q, kb, vt, lams, subln, lam_init=lam_init)
    y_p = _post(att_p, ssd_p, xp2, p_prompt.reshape(bp * lp, P_DIM), w4, tm=2 * tm, alpha=alpha)
    conv_p = xbc.reshape(bp, lp, CONV_DIM)[:, lp - (CONV_W - 1):, :]

    hist = CONV_W - 1
    x8 = jnp.zeros((bs, SROWS, D_MODEL), F32).at[:, hist:hist + ls].set(x_sample)
    pos8 = jnp.tile(jnp.clip(jnp.arange(SROWS) - hist, 0, ls - 1) + past_len, bs)
    qs, kfs, vfs, zs, xbcs, dts = _inproj(x8.reshape(bs * SROWS, D_MODEL), w1, _rope_tables(pos8),
                                          tm=256, nbatch=1, attn_copies=False)
    xbc8 = xbcs.reshape(bs, SROWS, CONV_DIM).at[:, :hist].set(state_conv.astype(F32))
    ssd_s8, ssm_s = _ssd_sample(xbc8.reshape(bs * SROWS, CONV_DIM), dts, zs,
                                state_ssm.reshape(bs, SSD_HEADS * SSD_HD, D_STATE), pssd)
    tok = lambda a, w: a.reshape(bs, SROWS, w)[:, hist:hist + ls]
    ncomp = 2 * ATT_HEADS
    q4 = tok(qs, 1024).reshape(bs, ls, ncomp, ATT_HD)
    qcq = jnp.transpose(q4, (0, 2, 1, 3)).reshape(bs, ncomp * ls, ATT_HD)
    slot = (jnp.arange(ncomp * ls)[:, None] // ls) == (jnp.arange(ncomp * ATT_HD)[None, :] // ATT_HD)
    qbd = jnp.where(slot[None], jnp.tile(qcq, (1, 1, ncomp)), jnp.zeros((), BF16))
    k_s = tok(kfs, 1024)
    v_s = tok(vfs, 1024)
    vexp = jnp.repeat(v_s.reshape(bs, ls, ATT_HEADS, 2 * ATT_HD), 2 * ls, axis=2)
    att_s = _decode_attn(qbd, kfs.reshape(bs, SROWS, 1024), vexp,
                         jnp.transpose(cache_k, (0, 2, 3, 1)),
                         cache_v.reshape(cache_v.shape[0], page * ATT_HEADS, 2 * ATT_HD),
                         page_table, lams, subln, lam_init=lam_init)
    att_s = att_s.reshape(bs, ATT_HEADS, 2, ls, LANES)[:, :, 0]
    att_s = jnp.transpose(att_s, (0, 2, 1, 3)).reshape(bs * ls, ATT_WIDTH)
    ssd_s = tok(ssd_s8, SSD_INNER).reshape(bs * ls, SSD_INNER)
    y_s = _post(att_s, ssd_s, x_sample.reshape(bs * ls, D_MODEL), p_sample.reshape(bs * ls, P_DIM), w4,
                tm=min(256, bs * ls), alpha=alpha)
    conv_s = tok(xbcs, CONV_DIM)[:, ls - hist:]

    return (y_p.reshape(bp, lp, D_MODEL), y_s.reshape(bs, ls, D_MODEL),
            k_p, vf.reshape(bp, lp, ATT_HEADS, 2 * ATT_HD),
            ssm_p.reshape(bp, SSD_HEADS, SSD_HD, D_STATE), conv_p,
            k_s.reshape(bs, ls, 2 * ATT_HEADS, ATT_HD), v_s.reshape(bs, ls, ATT_HEADS, 2 * ATT_HD),
            ssm_s.reshape(bs, SSD_HEADS, SSD_HD, D_STATE), conv_s)


def kernel(x_prompt, x_sample, p_prompt, p_sample, cache_k, cache_v, page_table, state_ssm, state_conv, w_in, conv_w, conv_b, dt_bias, a_log, d_skip, ssd_norm_w, lambda_q1, lambda_k1, lambda_q2, lambda_k2, subln_w, w_out, ln1_g, ln1_b, w_ffn_in, w_ffn_out, w_pe, w_pg, ln2_g, ln2_b):
    depth = w_in.shape[0]
    assert depth == 1, "single-layer trunk"
    prm = tuple(a[0] for a in (w_in, conv_w, conv_b, dt_bias, a_log, d_skip, ssd_norm_w, lambda_q1, lambda_k1,
                               lambda_q2, lambda_k2, subln_w, w_out, ln1_g, ln1_b, w_ffn_in, w_ffn_out, w_pe,
                               w_pg, ln2_g, ln2_b))
    lam_init = 0.8 - 0.6 * math.exp(-0.3 * 0)
    outs = _layer(depth, lam_init, x_prompt, x_sample, p_prompt[0], p_sample[0], cache_k[0], cache_v[0], page_table,
                  state_ssm[0], state_conv[0], prm)
    y_p, y_s = outs[0], outs[1]
    return (y_p, y_s) + tuple(o[None] for o in outs[2:])
```

```python
import functools
import math

import jax
import jax.numpy as jnp
from jax import lax
from jax.experimental import pallas as pl
from jax.experimental.pallas import tpu as pltpu

F32 = jnp.float32
BF16 = jnp.bfloat16

D_MODEL = 1024
ATT_HD = 64
ATT_HEADS = 8
ATT_WIDTH = 1024
ROT_DIM = 16
ROPE_THETA = 500000.0
SSD_HD = 64
SSD_INNER = 1024
SSD_HEADS = 16
SSD_GROUPS = 2
D_STATE = 128
CONV_W = 4
CONV_DIM = 1536
CHUNK = 128
D_FF = 2816
P_DIM = 256
EPS = 1e-5
LANES = 128
NEG = -1e30
Q_SCALE = (ATT_HD ** -0.5) * math.log2(math.e)

VMEM_LIMIT = 56 * 1024 * 1024


def _cparams(sem):
    return pltpu.CompilerParams(dimension_semantics=sem, vmem_limit_bytes=VMEM_LIMIT)


def _dot(a, b):
    return jnp.dot(a, b, preferred_element_type=F32)


def _dot_nt(a, b):
    return lax.dot_general(a, b, (((1,), (1,)), ((), ())), preferred_element_type=F32)


def _split3(a):
    hi = a.astype(BF16)
    r = a - hi.astype(F32)
    mid = r.astype(BF16)
    lo = (r - mid.astype(F32)).astype(BF16)
    return hi, mid, lo


def _dot_sel_lhs(sel3, a):
    hi, mid, lo = _split3(a)
    return _dot(sel3, jnp.concatenate([hi, mid, lo], axis=0))


def _dot_sel_rhs(a, sel3):
    hi, mid, lo = _split3(a)
    return _dot(jnp.concatenate([hi, mid, lo], axis=1), sel3)


def _silu(x):
    h = 0.5 * x
    return h + h * jnp.tanh(h)


def _softplus(x):
    return jnp.maximum(x, 0.0) + jnp.log1p(jnp.exp(-jnp.abs(x)))


def _rope_cols(t, ra, rb, rc):
    outs = []
    for j in range(t.shape[1] // LANES):
        c = t[:, j * LANES:(j + 1) * LANES]
        outs.append(c * ra + pltpu.roll(c, LANES - ROT_DIM // 2, 1) * rb + pltpu.roll(c, ROT_DIM // 2, 1) * rc)
    return outs


def _inproj_body(x_ref, wq_ref, wk_ref, wv_ref, wkt_ref, wvt_ref, wz_ref, wxbc_ref, wdt_ref, ra_ref, rb_ref, rc_ref,
                 ct_ref, st_ref, q_ref, kf_ref, vf_ref, z_ref, xbc_ref, dt_ref, *maybe_bf16, attn_copies):
    xb = x_ref[...].astype(BF16)
    ra = ra_ref[...]
    rb = rb_ref[...]
    rc = rc_ref[...]
    q = _dot(xb, wq_ref[...])
    for j, c in enumerate(_rope_cols(q, ra, rb, rc)):
        q_ref[:, j * LANES:(j + 1) * LANES] = (c * Q_SCALE).astype(BF16)
    k = _dot(xb, wk_ref[...])
    for j, c in enumerate(_rope_cols(k, ra, rb, rc)):
        if attn_copies:
            maybe_bf16[0][:, j * LANES:(j + 1) * LANES] = c.astype(BF16)
        else:
            kf_ref[:, j * LANES:(j + 1) * LANES] = c
    vf_ref[...] = _dot(xb, wv_ref[...])
    if attn_copies:
        maybe_bf16[1][0, 0] = _dot_nt(wvt_ref[...], xb).astype(BF16)
        kt = _dot_nt(wkt_ref[...], xb)
        kf_ref[0] = kt
        cos = ct_ref[...]
        sin = st_ref[...]
        half = ROT_DIM // 2
        for c in range(2 * ATT_HEADS):
            x1 = kt[c * ATT_HD:c * ATT_HD + half, :]
            x2 = kt[c * ATT_HD + half:c * ATT_HD + ROT_DIM, :]
            kf_ref[0, c * ATT_HD:c * ATT_HD + half, :] = x1 * cos - x2 * sin
            kf_ref[0, c * ATT_HD + half:c * ATT_HD + ROT_DIM, :] = x2 * cos + x1 * sin
    z_ref[...] = _dot(xb, wz_ref[...])
    xbc_ref[...] = _dot(xb, wxbc_ref[...])
    dt_ref[...] = _dot(xb, wdt_ref[...])


def _inproj(x2d, w, tabs, *, tm, nbatch, attn_copies):
    T = x2d.shape[0]
    nt = T // tm
    ntab = tabs[0].shape[0] // tm
    per_b = nt // nbatch
    full = lambda shape: pl.BlockSpec(shape, lambda i: (0,) * len(shape))
    row = lambda width: pl.BlockSpec((tm, width), lambda i: (i, 0))
    tab = pl.BlockSpec((tm, LANES), lambda i: (i % ntab, 0))
    tabt = pl.BlockSpec((ROT_DIM // 2, tm), lambda i: (0, i % ntab))
    in_specs = [row(D_MODEL), full((D_MODEL, 1024)), full((D_MODEL, 1024)), full((D_MODEL, 1024)),
                full((1024, D_MODEL)), full((1024, D_MODEL)), full((D_MODEL, 1024)), full((D_MODEL, CONV_DIM)),
                full((D_MODEL, LANES)), tab, tab, tab, tabt, tabt]
    k_shape, k_spec = jax.ShapeDtypeStruct((T, 1024), F32), row(1024)
    if attn_copies:
        k_shape = jax.ShapeDtypeStruct((nbatch, 1024, T // nbatch), F32)
        k_spec = pl.BlockSpec((1, 1024, tm), lambda i: (i // per_b, 0, i % per_b))
    out_shape = [jax.ShapeDtypeStruct((T, 1024), BF16), k_shape,
                 jax.ShapeDtypeStruct((T, 1024), F32), jax.ShapeDtypeStruct((T, 1024), F32),
                 jax.ShapeDtypeStruct((T, CONV_DIM), F32), jax.ShapeDtypeStruct((T, LANES), F32)]
    out_specs = [row(1024), k_spec, row(1024), row(1024), row(CONV_DIM), row(LANES)]
    if attn_copies:
        out_shape += [jax.ShapeDtypeStruct((T, 1024), BF16),
                      jax.ShapeDtypeStruct((nbatch, per_b, 1024, tm), BF16)]
        out_specs += [row(1024), pl.BlockSpec((1, 1, 1024, tm), lambda i: (i // per_b, i % per_b, 0, 0))]
    return pl.pallas_call(
        functools.partial(_inproj_body, attn_copies=attn_copies),
        grid=(nt,), in_specs=in_specs, out_specs=out_specs, out_shape=out_shape,
        compiler_params=_cparams(("parallel",)), name="inproj",
    )(x2d, w["wq"], w["wk"], w["wv"], w["wkt"], w["wvt"], w["wz"], w["wxbc"], w["wdt"], *tabs)


def _ssd_chunk_math(xpad_ref, dt_raw, lmat, mtot, e64, e128, convw, convb, dtb, alog):
    acc = convb
    for j in range(CONV_W):
        acc = acc + xpad_ref[pl.ds(8 - (CONV_W - 1) + j, CHUNK), :] * convw[j:j + 1, :]
    act = _silu(acc)
    xs = act[:, :SSD_INNER]
    lane = lax.broadcasted_iota(jnp.int32, (1, LANES), 1)
    a_row = jnp.where(lane < SSD_HEADS, -jnp.exp(alog), 0.0)
    dtv = _softplus(dt_raw + dtb)
    dA = dtv * a_row
    cs = _dot_sel_lhs(lmat, dA)
    cst = _dot_sel_lhs(mtot, dA)
    csT = cs.T
    ex = _dot_sel_rhs(jnp.concatenate([dtv, cs, cst], axis=0), e64)
    dt_e = ex[0:CHUNK]
    cs_e = ex[CHUNK:2 * CHUNK]
    cst_e = ex[2 * CHUNK:3 * CHUNK]
    cs_b = _dot_sel_rhs(cs, e128)
    X = xs * dt_e
    Xd = X * jnp.exp(cst_e - cs_e)
    ecs_e = jnp.exp(cs_e)
    lbool = lmat[:, 0:CHUNK].astype(F32) > 0.5
    lane2 = lax.broadcasted_iota(jnp.int32, (CHUNK, LANES), 1)
    ydiag = []
    bgs, cgs = [], []
    for g in range(SSD_GROUPS):
        bg = act[:, SSD_INNER + g * D_STATE:SSD_INNER + (g + 1) * D_STATE].astype(BF16)
        cg = act[:, SSD_INNER + (SSD_GROUPS + g) * D_STATE:SSD_INNER + (SSD_GROUPS + g + 1) * D_STATE].astype(BF16)
        bgs.append(bg)
        cgs.append(cg)
        cb = _dot_nt(cg, bg)
        for jj in range(SSD_HEADS // SSD_GROUPS // 2):
            h0 = g * (SSD_HEADS // SSD_GROUPS) + 2 * jj
            xp = X[:, h0 * SSD_HD:h0 * SSD_HD + LANES].astype(BF16)
            yp = []
            for h in (h0, h0 + 1):
                seg = cs_b[:, h * LANES:(h + 1) * LANES] - csT[h:h + 1, :]
                lm = jnp.exp(jnp.where(lbool, seg, NEG))
                yp.append(_dot((cb * lm).astype(BF16), xp))
            ydiag.append(jnp.where(lane2 < SSD_HD, yp[0], yp[1]))
    return dict(act=act, xs=xs, Xd=Xd, ecs_e=ecs_e, csT=csT, ydiag=ydiag, bgs=bgs, cgs=cgs)


def _ssd_finish(y_cols, xs, z, dskip, normw, out_ref):
    gz = []
    for j in range(SSD_INNER // LANES):
        sl = slice(j * LANES, (j + 1) * LANES)
        y = y_cols[j] + dskip[:, sl] * xs[:, sl]
        gz.append(y * _silu(z[:, sl]))
    per_g = SSD_INNER // SSD_GROUPS // LANES
    for g in range(SSD_GROUPS):
        blk = gz[g * per_g:(g + 1) * per_g]
        ss = sum(jnp.sum(b * b, axis=1, keepdims=True) for b in blk)
        inv = lax.rsqrt(ss / (SSD_INNER // SSD_GROUPS) + EPS)
        for j, b in enumerate(blk):
            sl = slice((g * per_g + j) * LANES, (g * per_g + j + 1) * LANES)
            out_ref[:, sl] = (b * inv * normw[:, sl]).astype(out_ref.dtype)


def _ssd_prompt_body(xbc_ref, dt_ref, z_ref, lmat_ref, mtot_ref, e64_ref, e128_ref, convw_ref, convb_ref,
                     dtb_ref, alog_ref, dskip_ref, normw_ref, out_ref, state_ref, xpad_ref):
    c = pl.program_id(1)

    @pl.when(c == 0)
    def _():
        xpad_ref[pl.ds(0, 8), :] = jnp.zeros((8, CONV_DIM), F32)
        state_ref[...] = jnp.zeros(state_ref.shape, F32)

    xpad_ref[pl.ds(8, CHUNK), :] = xbc_ref[...]
    m = _ssd_chunk_math(xpad_ref, dt_ref[...], lmat_ref[...], mtot_ref[...], e64_ref[...], e128_ref[...],
                        convw_ref[...], convb_ref[...], dtb_ref[...], alog_ref[...])
    xpad_ref[pl.ds(8 - (CONV_W - 1), CONV_W - 1), :] = xpad_ref[pl.ds(8 + CHUNK - (CONV_W - 1), CONV_W - 1), :]

    csT = m["csT"]
    cd = jnp.exp(jnp.broadcast_to(csT[0:SSD_HEADS, CHUNK - 1:CHUNK], (SSD_HEADS, LANES)))
    hpg = SSD_HEADS // SSD_GROUPS
    gw = hpg * SSD_HD
    y_cols = []
    for g in range(SSD_GROUPS):
        sg = state_ref[0, pl.ds(g * gw, gw), :]
        yoff = _dot_nt(m["cgs"][g], sg.astype(BF16))
        for jj in range(gw // LANES):
            col = g * (gw // LANES) + jj
            y_cols.append(m["ydiag"][col] + yoff[:, jj * LANES:(jj + 1) * LANES]
                          * m["ecs_e"][:, col * LANES:(col + 1) * LANES])
        xdg_t = m["Xd"][:, g * gw:(g + 1) * gw].T.astype(BF16)
        contrib = _dot(xdg_t, m["bgs"][g])
        for hh in range(hpg):
            h = g * hpg + hh
            rows = pl.ds(h * SSD_HD, SSD_HD)
            state_ref[0, rows, :] = (state_ref[0, rows, :] * cd[h:h + 1, :]
                                     + contrib[hh * SSD_HD:(hh + 1) * SSD_HD, :])
    _ssd_finish(y_cols, m["xs"], z_ref[...], dskip_ref[...], normw_ref[...], out_ref)


def _ssd_consts(p):
    return [p["e64"], p["e128"], p["conv_w"], p["conv_b"], p["dt_bias"], p["a_log"], p["d_skip"], p["norm_w"]]


def _const_specs(nd_grid):
    z = (0, 0)
    if nd_grid == 2:
        f = lambda shape: pl.BlockSpec(shape, lambda b, c: z)
    else:
        f = lambda shape: pl.BlockSpec(shape, lambda i: z)
    return [f((3 * LANES, SSD_INNER)), f((3 * LANES, SSD_HEADS * LANES)), f((CONV_W, CONV_DIM)), f((1, CONV_DIM)),
            f((1, LANES)), f((1, LANES)), f((1, SSD_INNER)), f((1, SSD_INNER))]


def _ssd_prompt(xbc, dt, z, p, nbatch):
    T = xbc.shape[0]
    nc = T // nbatch // CHUNK
    row = lambda width: pl.BlockSpec((CHUNK, width), lambda b, c: (b * nc + c, 0))
    sq = pl.BlockSpec((CHUNK, 3 * CHUNK), lambda b, c: (0, 0))
    tri = jnp.tile(jnp.tril(jnp.ones((CHUNK, CHUNK), F32)).astype(BF16), (1, 3))
    ones = jnp.ones((CHUNK, 3 * CHUNK), BF16)
    return pl.pallas_call(
        _ssd_prompt_body,
        grid=(nbatch, nc),
        in_specs=[row(CONV_DIM), row(LANES), row(SSD_INNER), sq, sq] + _const_specs(2),
        out_specs=[row(SSD_INNER), pl.BlockSpec((1, SSD_HEADS * SSD_HD, D_STATE), lambda b, c: (b, 0, 0))],
        out_shape=[jax.ShapeDtypeStruct((T, SSD_INNER), BF16),
                   jax.ShapeDtypeStruct((nbatch, SSD_HEADS * SSD_HD, D_STATE), F32)],
        scratch_shapes=[pltpu.VMEM((8 + CHUNK, CONV_DIM), F32)],
        compiler_params=_cparams(("parallel", "arbitrary")), name="ssd_prompt",
    )(xbc, dt, z, tri, ones, *_ssd_consts(p))


SROWS = 8
SB = CHUNK // SROWS


def _ssd_sample_body(xbc_ref, dt_ref, z_ref, s0_ref, lmat_ref, mtot_ref, e64_ref, e128_ref, convw_ref, convb_ref,
                     dtb_ref, alog_ref, dskip_ref, normw_ref, out_ref, s1_ref, xpad_ref):
    xpad_ref[pl.ds(0, 8), :] = jnp.zeros((8, CONV_DIM), F32)
    xpad_ref[pl.ds(8, CHUNK), :] = xbc_ref[...]
    m = _ssd_chunk_math(xpad_ref, dt_ref[...], lmat_ref[...], mtot_ref[...], e64_ref[...], e128_ref[...],
                        convw_ref[...], convb_ref[...], dtb_ref[...], alog_ref[...])
    csT = m["csT"]
    hpg = SSD_HEADS // SSD_GROUPS
    gw = hpg * SSD_HD
    rowi = lax.broadcasted_iota(jnp.int32, (CHUNK, LANES), 0)
    last_tok = CONV_W - 1 + DEC_Q - 1
    xdT = [m["Xd"][:, g * gw:(g + 1) * gw].T.astype(BF16) for g in range(SSD_GROUPS)]
    yoff = [[jnp.zeros((CHUNK, LANES), F32) for _ in range(gw // LANES)] for _ in range(SSD_GROUPS)]
    for b in range(SB):
        lo = b * SROWS + (CONV_W - 1)
        tok = (rowi >= lo) & (rowi <= b * SROWS + last_tok)
        col = b * SROWS + last_tok
        cd = jnp.exp(jnp.broadcast_to(csT[0:SSD_HEADS, col:col + 1], (SSD_HEADS, LANES)))
        for g in range(SSD_GROUPS):
            sg = s0_ref[b, pl.ds(g * gw, gw), :]
            res = _dot_nt(m["cgs"][g], sg.astype(BF16))
            for jj in range(gw // LANES):
                yoff[g][jj] = jnp.where(tok, res[:, jj * LANES:(jj + 1) * LANES], yoff[g][jj])
            bmask = jnp.where(tok, m["bgs"][g], jnp.zeros_like(m["bgs"][g]))
            contrib = _dot(xdT[g], bmask)
            for hh in range(hpg):
                h = g * hpg + hh
                rows = pl.ds(h * SSD_HD, SSD_HD)
                s1_ref[b, rows, :] = (s0_ref[b, rows, :] * cd[h:h + 1, :]
                                      + contrib[hh * SSD_HD:(hh + 1) * SSD_HD, :])
    y_cols = []
    for g in range(SSD_GROUPS):
        for jj in range(gw // LANES):
            col = g * (gw // LANES) + jj
            y_cols.append(m["ydiag"][col] + yoff[g][jj] * m["ecs_e"][:, col * LANES:(col + 1) * LANES])
    _ssd_finish(y_cols, m["xs"], z_ref[...], dskip_ref[...], normw_ref[...], out_ref)


def _ssd_sample(xbc8, dt8, z8, s0, p):
    T8 = xbc8.shape[0]
    nblk = T8 // CHUNK
    row = lambda width: pl.BlockSpec((CHUNK, width), lambda i: (i, 0))
    sq = pl.BlockSpec((CHUNK, 3 * CHUNK), lambda i: (0, 0))
    st = pl.BlockSpec((SB, SSD_HEADS * SSD_HD, D_STATE), lambda i: (i, 0, 0))
    r = jnp.arange(CHUNK)
    same = (r[:, None] // SROWS) == (r[None, :] // SROWS)
    is_tok = ((r % SROWS) >= CONV_W - 1) & ((r % SROWS) < SROWS - 1)
    lmat = jnp.tile((same & (r[None, :] <= r[:, None]) & is_tok[None, :] & is_tok[:, None]).astype(BF16), (1, 3))
    mtot = jnp.tile((same & is_tok[None, :]).astype(BF16), (1, 3))
    return pl.pallas_call(
        _ssd_sample_body,
        grid=(nblk,),
        in_specs=[row(CONV_DIM), row(LANES), row(SSD_INNER), st, sq, sq] + _const_specs(1),
        out_specs=[row(SSD_INNER), st],
        out_shape=[jax.ShapeDtypeStruct((T8, SSD_INNER), BF16), jax.ShapeDtypeStruct(s0.shape, F32)],
        scratch_shapes=[pltpu.VMEM((8 + CHUNK, CONV_DIM), F32)],
        compiler_params=_cparams(("parallel",)), name="ssd_sample",
    )(xbc8, dt8, z8, s0, lmat, mtot, *_ssd_consts(p))


def _lam_value(lq1, lk1, lq2, lk2, lam_init):
    s1 = jnp.sum(lq1[...] * lk1[...], axis=1, keepdims=True)
    s2 = jnp.sum(lq2[...] * lk2[...], axis=1, keepdims=True)
    return jnp.exp(s1) - jnp.exp(s2) + lam_init


def _flash_body(q_ref, k_ref, vt_ref, lq1, lk1, lq2, lk2, sw_ref, o_ref, sa_ref, sb_ref, bma_ref, bmb_ref,
                m_ref, acc_ref, *, tq, tk, lam_init):
    i = pl.program_id(2)
    qt = q_ref[...].astype(F32).T
    row = lax.broadcasted_iota(jnp.int32, (LANES, tq), 0)
    qst = jnp.concatenate([jnp.where(row < ATT_HD, qt, 0.0), jnp.where(row >= ATT_HD, qt, 0.0)],
                          axis=1).astype(BF16)
    ones = jnp.ones((ONES_ROWS, tk), BF16)
    m_ref[...] = jnp.full(m_ref.shape, NEG, F32)
    acc_ref[...] = jnp.zeros(acc_ref.shape, F32)

    def produce(blk, s_ref, bm_ref):
        kblk = k_ref[pl.ds(pl.multiple_of(blk * tk, tk), tk), :]
        s = _dot(kblk, qst)
        s_ref[...] = s
        bm_ref[...] = jnp.max(s, axis=0, keepdims=True)

    def consume(blk, s_ref, bm_ref, off):
        s = s_ref[...]
        if off is None:
            bmax = bm_ref[...]
        else:
            kpos = lax.broadcasted_iota(jnp.int32, (tk, 2 * tq), 0) + off
            qpos = lax.broadcasted_iota(jnp.int32, (tk, 2 * tq), 1)
            qpos = jnp.where(qpos >= tq, qpos - tq, qpos)
            s = jnp.where(kpos <= qpos, s, NEG)
            bmax = jnp.max(s, axis=0, keepdims=True)
        m_old = m_ref[...]
        m_new = jnp.maximum(m_old, bmax)
        alpha = jnp.exp2(m_old - m_new)
        p = jnp.exp2((s - m_new).astype(BF16))
        m_ref[...] = m_new
        vt1 = jnp.concatenate([vt_ref[0, blk], ones], axis=0)
        acc_ref[...] = alpha * acc_ref[...] + _dot(vt1, p)

    produce(0, sa_ref, bma_ref)

    def pair(p):
        produce(2 * p + 1, sb_ref, bmb_ref)
        consume(2 * p, sa_ref, bma_ref, None)
        produce(2 * p + 2, sa_ref, bma_ref)
        consume(2 * p + 1, sb_ref, bmb_ref, None)

    def body(pp, carry):
        pair(2 * pp)
        pair(2 * pp + 1)
        return carry

    lax.fori_loop(0, i // 2, body, 0)

    @pl.when(i % 2 == 1)
    def _():
        pair(i - 1)

    produce(2 * i + 1, sb_ref, bmb_ref)
    consume(2 * i, sa_ref, bma_ref, 0)
    consume(2 * i + 1, sb_ref, bmb_ref, tk)

    lam = _lam_value(lq1, lk1, lq2, lk2, lam_init)
    acc = acc_ref[0:LANES, :]
    l = acc_ref[LANES:LANES + 1, :]
    o = acc[:, :tq] / l[:, :tq] - lam * (acc[:, tq:] / l[:, tq:])
    inv = lax.rsqrt(jnp.mean(o * o, axis=0, keepdims=True) + EPS)
    o_ref[...] = ((o * inv).T * (sw_ref[...] * (1.0 - lam_init))).astype(o_ref.dtype)


def _flash(q, kb, vt, lams, subln, *, lam_init):
    T = q.shape[0]
    nb, nk, _, tk = vt.shape
    tq = 2 * tk
    L = T // nb
    nq = L // tq
    vec = pl.BlockSpec((1, ATT_HD), lambda b, h, i: (0, 0))
    return pl.pallas_call(
        functools.partial(_flash_body, tq=tq, tk=tk, lam_init=lam_init),
        grid=(nb, ATT_HEADS, nq),
        in_specs=[pl.BlockSpec((tq, LANES), lambda b, h, i: (b * nq + i, h)),
                  pl.BlockSpec((L, LANES), lambda b, h, i: (b, h)),
                  pl.BlockSpec((1, nk, LANES, tk), lambda b, h, i: (b, 0, h, 0)),
                  vec, vec, vec, vec, pl.BlockSpec((1, LANES), lambda b, h, i: (0, 0))],
        out_specs=pl.BlockSpec((tq, LANES), lambda b, h, i: (b * nq + i, h)),
        out_shape=jax.ShapeDtypeStruct((T, ATT_WIDTH), BF16),
        scratch_shapes=[pltpu.VMEM((tk, 2 * tq), F32), pltpu.VMEM((tk, 2 * tq), F32),
                        pltpu.VMEM((1, 2 * tq), F32), pltpu.VMEM((1, 2 * tq), F32),
                        pltpu.VMEM((1, 2 * tq), F32), pltpu.VMEM((LANES + ONES_ROWS, 2 * tq), F32)],
        compiler_params=_cparams(("parallel", "parallel", "arbitrary")), name="flash_diff",
    )(q, kb, vt, *lams, subln)


ONES_ROWS = 16
PAGES_PER_STEP = 8
DEC_Q = 4


def _decode_body(pt_ref, qbd_ref, kn_ref, vx_ref, *refs, pps, page, lam_init):
    k_refs = refs[:pps]
    v_refs = refs[pps:2 * pps]
    lq1, lk1, lq2, lk2, sw_ref, o_ref, m_ref, l_ref, acc_ref = refs[2 * pps:]
    j = pl.program_id(1)
    nsteps = pl.num_programs(1)
    rows_per_head = 2 * DEC_Q

    @pl.when(j == 0)
    def _():
        m_ref[...] = jnp.full(m_ref.shape, NEG, F32)
        l_ref[...] = jnp.zeros(l_ref.shape, F32)
        acc_ref[...] = jnp.zeros(acc_ref.shape, F32)

    qbd = qbd_ref[0]
    kk = jnp.concatenate([k_refs[i][0].reshape(2 * ATT_HEADS * ATT_HD, page).astype(BF16) for i in range(pps)],
                         axis=1)
    s = _dot(qbd, kk)
    m_old = m_ref[...]
    m_new = jnp.maximum(m_old, jnp.max(s, axis=1, keepdims=True))
    alpha = jnp.exp2(m_old - m_new)
    p = jnp.exp2(s - m_new)
    l_ref[...] = alpha * l_ref[...] + jnp.sum(p, axis=1, keepdims=True)
    m_ref[...] = m_new
    pb = p.astype(BF16)

    def head_values(h):
        return jnp.concatenate([v_refs[i][0, pl.ds(h, page, stride=ATT_HEADS), :] for i in range(pps)],
                               axis=0).astype(BF16)

    for h in range(0, ATT_HEADS, 2):
        vv = jnp.concatenate([head_values(h), head_values(h + 1)], axis=1)
        r0 = h * rows_per_head
        pv = _dot(pb[r0:r0 + 2 * rows_per_head, :], vv)
        for u in range(2):
            rows = pl.ds(r0 + u * rows_per_head, rows_per_head)
            acc_ref[rows, :] = (alpha[r0 + u * rows_per_head:r0 + (u + 1) * rows_per_head, :] * acc_ref[rows, :]
                                + pv[u * rows_per_head:(u + 1) * rows_per_head, u * LANES:(u + 1) * LANES])

    @pl.when(j == nsteps - 1)
    def _():
        hist = CONV_W - 1
        sn = _dot_nt(qbd, kn_ref[0].astype(BF16))
        r = lax.broadcasted_iota(jnp.int32, sn.shape, 0)
        t = lax.broadcasted_iota(jnp.int32, sn.shape, 1) - hist
        sn = jnp.where((t >= 0) & (t < DEC_Q) & (t <= r % DEC_Q), sn, NEG)
        m_o = m_ref[...]
        m_n = jnp.maximum(m_o, jnp.max(sn, axis=1, keepdims=True))
        a = jnp.exp2(m_o - m_n)
        pn = jnp.exp2(sn - m_n)
        l = a * l_ref[...] + jnp.sum(pn, axis=1, keepdims=True)
        acc = a * acc_ref[...]
        for tt in range(DEC_Q):
            acc = acc + pn[:, hist + tt:hist + tt + 1] * vx_ref[0, tt]
        lam = _lam_value(lq1, lk1, lq2, lk2, lam_init)
        o1 = acc / l
        nrow = o1.shape[0]
        o = o1 - lam * pltpu.roll(o1, nrow - DEC_Q, 0)
        inv = lax.rsqrt(jnp.mean(o * o, axis=1, keepdims=True) + EPS)
        o_ref[0] = (o * inv * (sw_ref[...] * (1.0 - lam_init))).astype(o_ref.dtype)


def _decode_attn(qbd, knew, vexp, ckt, cv, page_table, lams, subln, *, lam_init):
    nb, npages = page_table.shape
    page = ckt.shape[3]
    pps = PAGES_PER_STEP
    nsteps = npages // pps
    nrow = 2 * ATT_HEADS * DEC_Q

    def kspec(i):
        return pl.BlockSpec((1, 2 * ATT_HEADS, ATT_HD, page), lambda b, j, pt: (pt[b, j * pps + i], 0, 0, 0))

    def vspec(i):
        return pl.BlockSpec((1, page * ATT_HEADS, LANES), lambda b, j, pt: (pt[b, j * pps + i], 0, 0))

    vec = pl.BlockSpec((1, ATT_HD), lambda b, j, pt: (0, 0))
    in_specs = ([pl.BlockSpec((1, nrow, 2 * ATT_HEADS * ATT_HD), lambda b, j, pt: (b, 0, 0)),
                 pl.BlockSpec((1, SROWS, 2 * ATT_HEADS * ATT_HD), lambda b, j, pt: (b, 0, 0)),
                 pl.BlockSpec((1, DEC_Q, nrow, LANES), lambda b, j, pt: (b, 0, 0, 0))]
                + [kspec(i) for i in range(pps)] + [vspec(i) for i in range(pps)]
                + [vec, vec, vec, vec, pl.BlockSpec((1, LANES), lambda b, j, pt: (0, 0))])
    grid_spec = pltpu.PrefetchScalarGridSpec(
        num_scalar_prefetch=1, grid=(nb, nsteps), in_specs=in_specs,
        out_specs=pl.BlockSpec((1, nrow, LANES), lambda b, j, pt: (b, 0, 0)),
        scratch_shapes=[pltpu.VMEM((nrow, 1), F32), pltpu.VMEM((nrow, 1), F32), pltpu.VMEM((nrow, LANES), F32)])
    return pl.pallas_call(
        functools.partial(_decode_body, pps=pps, page=page, lam_init=lam_init),
        grid_spec=grid_spec,
        out_shape=jax.ShapeDtypeStruct((nb, nrow, LANES), BF16),
        compiler_params=_cparams(("parallel", "arbitrary")), name="decode_attn",
    )(page_table, qbd, knew, vexp, *([ckt] * pps), *([cv] * pps), *lams, subln)


def _layer_norm(x, g, b):
    mu = jnp.mean(x, axis=1, keepdims=True)
    xc = x - mu
    var = jnp.mean(xc * xc, axis=1, keepdims=True)
    return xc * lax.rsqrt(var + EPS) * g + b


def _post_body(att_ref, ssd_ref, x_ref, pe_ref, woa_ref, wos_ref, wg_ref, wu_ref, wfo_ref, wpe_ref, wpg_ref,
               g1_ref, b1_ref, g2_ref, b2_ref, o_ref, *, alpha, nff):
    mix = _dot(att_ref[...], woa_ref[...]) + _dot(ssd_ref[...], wos_ref[...])
    h = _layer_norm(alpha * x_ref[...] + mix, g1_ref[...], b1_ref[...])
    hb = h.astype(BF16)
    ffn = None
    step = D_FF // nff
    for c in range(nff):
        sl = slice(c * step, (c + 1) * step)
        gate = _dot(hb, wg_ref[:, sl])
        up = _dot(hb, wu_ref[:, sl])
        part = _dot((_silu(gate) * up).astype(BF16), wfo_ref[sl, :])
        ffn = part if ffn is None else ffn + part
    pemb = _dot(pe_ref[...].astype(BF16), wpe_ref[...]) * jax.nn.sigmoid(_dot(hb, wpg_ref[...]))
    o_ref[...] = _layer_norm(alpha * h + ffn + pemb, g2_ref[...], b2_ref[...])


def _post(att, ssd, x2d, pe2d, w, *, tm, alpha):
    T = x2d.shape[0]
    row = lambda width: pl.BlockSpec((tm, width), lambda i: (i, 0))
    full = lambda shape: pl.BlockSpec(shape, lambda i: (0, 0), pipeline_mode=pl.Buffered(1))
    return pl.pallas_call(
        functools.partial(_post_body, alpha=alpha, nff=2),
        grid=(T // tm,),
        in_specs=[row(1024), row(1024), row(D_MODEL), row(P_DIM),
                  full((1024, D_MODEL)), full((1024, D_MODEL)), full((D_MODEL, D_FF)), full((D_MODEL, D_FF)),
                  full((D_FF, D_MODEL)), full((P_DIM, D_MODEL)), full((D_MODEL, D_MODEL)),
                  full((1, D_MODEL)), full((1, D_MODEL)), full((1, D_MODEL)), full((1, D_MODEL))],
        out_specs=row(D_MODEL),
        out_shape=jax.ShapeDtypeStruct((T, D_MODEL), F32),
        compiler_params=_cparams(("parallel",)), name="post",
    )(att, ssd, x2d, pe2d, w["woa"], w["wos"], w["wg"], w["wu"], w["wfo"], w["wpe"], w["wpg"],
      w["g1"], w["b1"], w["g2"], w["b2"])


def _rope_tables(pos):
    half = ROT_DIM // 2
    inv = ROPE_THETA ** (-jnp.arange(half, dtype=F32) * 2.0 / ROT_DIM)
    ang = pos.astype(F32)[:, None] * inv[None, :]
    cos, sin = jnp.cos(ang), jnp.sin(ang)
    n = pos.shape[0]
    pad = jnp.zeros((n, ATT_HD - ROT_DIM), F32)
    ra = jnp.concatenate([cos, cos, jnp.ones((n, ATT_HD - ROT_DIM), F32)], 1)
    rb = jnp.concatenate([-sin, jnp.zeros((n, half), F32), pad], 1)
    rc = jnp.concatenate([jnp.zeros((n, half), F32), sin, pad], 1)
    rep = LANES // ATT_HD
    return tuple(jnp.tile(t, (1, rep)) for t in (ra, rb, rc)) + (cos.T, sin.T)


def _layer(depth, lam_init, x_prompt, x_sample, p_prompt, p_sample, cache_k, cache_v, page_table, state_ssm,
           state_conv, prm):
    (w_in, conv_w, conv_b, dt_bias, a_log, d_skip, ssd_norm_w, lq1, lk1, lq2, lk2, subln_w,
     w_out, ln1_g, ln1_b, w_ffn_in, w_ffn_out, w_pe, w_pg, ln2_g, ln2_b) = prm
    bp, lp, _ = x_prompt.shape
    bs, ls, _ = x_sample.shape
    npages = page_table.shape[1]
    page = cache_k.shape[1]
    past_len = npages * page
    alpha = (2 * depth) ** 0.25

    wb = w_in.astype(BF16)
    wq = wb[:, 0:1024]
    wk = wb[:, 1024:2048]
    wv = wb[:, 2048:3072]
    wz = wb[:, 3072:4096]
    wxbc = wb[:, 4096:4096 + CONV_DIM]
    wdt = jnp.pad(wb[:, 4096 + CONV_DIM:], ((0, 0), (0, LANES - SSD_HEADS)))
    w1 = dict(wq=wq, wk=wk, wv=wv, wkt=wk.T, wvt=wv.T, wz=wz, wxbc=wxbc, wdt=wdt)

    hrep = jnp.arange(SSD_HEADS * SSD_HD) // SSD_HD
    e64 = (jnp.arange(LANES)[:, None] == hrep[None, :]).astype(BF16)
    e128 = (jnp.arange(LANES)[:, None] == (jnp.arange(SSD_HEADS * LANES) // LANES)[None, :]).astype(BF16)
    padl = lambda v: jnp.pad(v.astype(F32), (0, LANES - v.shape[0]))[None, :]
    pssd = dict(e64=jnp.tile(e64, (3, 1)), e128=jnp.tile(e128, (3, 1)), conv_w=conv_w.astype(F32), conv_b=conv_b.astype(F32)[None, :],
                dt_bias=padl(dt_bias), a_log=padl(a_log), d_skip=jnp.repeat(d_skip.astype(F32), SSD_HD)[None, :],
                norm_w=ssd_norm_w.astype(F32)[None, :])
    lams = [v.astype(F32)[None, :] for v in (lq1, lk1, lq2, lk2)]
    subln = subln_w.astype(F32)[None, :]
    wob = w_out.astype(BF16)
    wfi = w_ffn_in.astype(BF16)
    w4 = dict(woa=wob[:ATT_WIDTH], wos=wob[ATT_WIDTH:], wg=wfi[:, :D_FF], wu=wfi[:, D_FF:],
              wfo=w_ffn_out.astype(BF16), wpe=w_pe.astype(BF16), wpg=w_pg.astype(BF16),
              g1=ln1_g.astype(F32)[None, :], b1=ln1_b.astype(F32)[None, :],
              g2=ln2_g.astype(F32)[None, :], b2=ln2_b.astype(F32)[None, :])

    tm = 256
    xp2 = x_prompt.reshape(bp * lp, D_MODEL)
    q, kt, vf, z, xbc, dt, kb, vt = _inproj(xp2, w1, _rope_tables(jnp.arange(lp)), tm=tm, nbatch=bp,
                                            attn_copies=True)
    k_p = jnp.transpose(kt.reshape(bp, 2 * ATT_HEADS, ATT_HD, lp), (0, 3, 1, 2))
    ssd_p, ssm_p = _ssd_prompt(xbc, dt, z, pssd, bp)
    att_p = _flash(q, kb, vt, lams, subln, lam_init=lam_init)
    y_p = _post(att_p, ssd_p, xp2, p_prompt.reshape(bp * lp, P_DIM), w4, tm=2 * tm, alpha=alpha)
    conv_p = xbc.reshape(bp, lp, CONV_DIM)[:, lp - (CONV_W - 1):, :]

    hist = CONV_W - 1
    x8 = jnp.zeros((bs, SROWS, D_MODEL), F32).at[:, hist:hist + ls].set(x_sample)
    pos8 = jnp.tile(jnp.clip(jnp.arange(SROWS) - hist, 0, ls - 1) + past_len, bs)
    qs, kfs, vfs, zs, xbcs, dts = _inproj(x8.reshape(bs * SROWS, D_MODEL), w1, _rope_tables(pos8),
                                          tm=256, nbatch=1, attn_copies=False)
    xbc8 = xbcs.reshape(bs, SROWS, CONV_DIM).at[:, :hist].set(state_conv.astype(F32))
    ssd_s8, ssm_s = _ssd_sample(xbc8.reshape(bs * SROWS, CONV_DIM), dts, zs,
                                state_ssm.reshape(bs, SSD_HEADS * SSD_HD, D_STATE), pssd)
    tok = lambda a, w: a.reshape(bs, SROWS, w)[:, hist:hist + ls]
    ncomp = 2 * ATT_HEADS
    q4 = tok(qs, 1024).reshape(bs, ls, ncomp, ATT_HD)
    qcq = jnp.transpose(q4, (0, 2, 1, 3)).reshape(bs, ncomp * ls, ATT_HD)
    slot = (jnp.arange(ncomp * ls)[:, None] // ls) == (jnp.arange(ncomp * ATT_HD)[None, :] // ATT_HD)
    qbd = jnp.where(slot[None], jnp.tile(qcq, (1, 1, ncomp)), jnp.zeros((), BF16))
    k_s = tok(kfs, 1024)
    v_s = tok(vfs, 1024)
    vexp = jnp.repeat(v_s.reshape(bs, ls, ATT_HEADS, 2 * ATT_HD), 2 * ls, axis=2)
    att_s = _decode_attn(qbd, kfs.reshape(bs, SROWS, 1024), vexp,
                         jnp.transpose(cache_k, (0, 2, 3, 1)),
                         cache_v.reshape(cache_v.shape[0], page * ATT_HEADS, 2 * ATT_HD),
                         page_table, lams, subln, lam_init=lam_init)
    att_s = att_s.reshape(bs, ATT_HEADS, 2, ls, LANES)[:, :, 0]
    att_s = jnp.transpose(att_s, (0, 2, 1, 3)).reshape(bs * ls, ATT_WIDTH)
    ssd_s = tok(ssd_s8, SSD_INNER).reshape(bs * ls, SSD_INNER)
    y_s = _post(att_s, ssd_s, x_sample.reshape(bs * ls, D_MODEL), p_sample.reshape(bs * ls, P_DIM), w4,
                tm=min(256, bs * ls), alpha=alpha)
    conv_s = tok(xbcs, CONV_DIM)[:, ls - hist:]

    return (y_p.reshape(bp, lp, D_MODEL), y_s.reshape(bs, ls, D_MODEL),
            k_p, vf.reshape(bp, lp, ATT_HEADS, 2 * ATT_HD),
            ssm_p.reshape(bp, SSD_HEADS, SSD_HD, D_STATE), conv_p,
            k_s.reshape(bs, ls, 2 * ATT_HEADS, ATT_HD), v_s.reshape(bs, ls, ATT_HEADS, 2 * ATT_HD),
            ssm_s.reshape(bs, SSD_HEADS, SSD_HD, D_STATE), conv_s)


def kernel(x_prompt, x_sample, p_prompt, p_sample, cache_k, cache_v, page_table, state_ssm, state_conv, w_in, conv_w, conv_b, dt_bias, a_log, d_skip, ssd_norm_w, lambda_q1, lambda_k1, lambda_q2, lambda_k2, subln_w, w_out, ln1_g, ln1_b, w_ffn_in, w_ffn_out, w_pe, w_pg, ln2_g, ln2_b):
    depth = w_in.shape[0]
    assert depth == 1, "single-layer trunk"
    prm = tuple(a[0] for a in (w_in, conv_w, conv_b, dt_bias, a_log, d_skip, ssd_norm_w, lambda_q1, lambda_k1,
                               lambda_q2, lambda_k2, subln_w, w_out, ln1_g, ln1_b, w_ffn_in, w_ffn_out, w_pe,
                               w_pg, ln2_g, ln2_b))
    lam_init = 0.8 - 0.6 * math.exp(-0.3 * 0)
    outs = _layer(depth, lam_init, x_prompt, x_sample, p_prompt[0], p_sample[0], cache_k[0], cache_v[0], page_table,
                  state_ssm[0], state_conv[0], prm)
    y_p, y_s = outs[0], outs[1]
    return (y_p, y_s) + tuple(o[None] for o in outs[2:])
```

```python
import functools
import math

import jax
import jax.numpy as jnp
from jax import lax
from jax.experimental import pallas as pl
from jax.experimental.pallas import tpu as pltpu

F32 = jnp.float32
BF16 = jnp.bfloat16

D_MODEL = 1024
ATT_HD = 64
ATT_HEADS = 8
ATT_WIDTH = 1024
ROT_DIM = 16
ROPE_THETA = 500000.0
SSD_HD = 64
SSD_INNER = 1024
SSD_HEADS = 16
SSD_GROUPS = 2
D_STATE = 128
CONV_W = 4
CONV_DIM = 1536
CHUNK = 128
D_FF = 2816
P_DIM = 256
EPS = 1e-5
LANES = 128
NEG = -1e30
Q_SCALE = (ATT_HD ** -0.5) * math.log2(math.e)

VMEM_LIMIT = 56 * 1024 * 1024


def _cparams(sem):
    return pltpu.CompilerParams(dimension_semantics=sem, vmem_limit_bytes=VMEM_LIMIT)


def _dot(a, b):
    return jnp.dot(a, b, preferred_element_type=F32)


def _dot_nt(a, b):
    return lax.dot_general(a, b, (((1,), (1,)), ((), ())), preferred_element_type=F32)


def _split3(a):
    hi = a.astype(BF16)
    r = a - hi.astype(F32)
    mid = r.astype(BF16)
    lo = (r - mid.astype(F32)).astype(BF16)
    return hi, mid, lo


def _dot_sel_lhs(sel3, a):
    hi, mid, lo = _split3(a)
    return _dot(sel3, jnp.concatenate([hi, mid, lo], axis=0))


def _dot_sel_rhs(a, sel3):
    hi, mid, lo = _split3(a)
    return _dot(jnp.concatenate([hi, mid, lo], axis=1), sel3)


def _silu(x):
    h = 0.5 * x
    return h + h * jnp.tanh(h)


def _softplus(x):
    return jnp.maximum(x, 0.0) + jnp.log1p(jnp.exp(-jnp.abs(x)))


def _rope_cols(t, ra, rb, rc):
    outs = []
    for j in range(t.shape[1] // LANES):
        c = t[:, j * LANES:(j + 1) * LANES]
        outs.append(c * ra + pltpu.roll(c, LANES - ROT_DIM // 2, 1) * rb + pltpu.roll(c, ROT_DIM // 2, 1) * rc)
    return outs


def _inproj_body(x_ref, wq_ref, wk_ref, wv_ref, wz_ref, wxbc_ref, wdt_ref, ra_ref, rb_ref, rc_ref,
                 q_ref, kf_ref, vf_ref, z_ref, xbc_ref, dt_ref, *maybe_bf16, attn_copies):
    xb = x_ref[...].astype(BF16)
    ra = ra_ref[...]
    rb = rb_ref[...]
    rc = rc_ref[...]
    q = _dot(xb, wq_ref[...])
    for j, c in enumerate(_rope_cols(q, ra, rb, rc)):
        q_ref[:, j * LANES:(j + 1) * LANES] = (c * Q_SCALE).astype(BF16)
    k = _dot(xb, wk_ref[...])
    for j, c in enumerate(_rope_cols(k, ra, rb, rc)):
        if attn_copies:
            maybe_bf16[0][:, j * LANES:(j + 1) * LANES] = c.astype(BF16)
            kf_ref[0, j * LANES:(j + 1) * LANES, :] = c.T
        else:
            kf_ref[:, j * LANES:(j + 1) * LANES] = c
    v = _dot(xb, wv_ref[...])
    vf_ref[...] = v
    if attn_copies:
        for j in range(v.shape[1] // LANES):
            maybe_bf16[1][0, 0, j * LANES:(j + 1) * LANES, :] = v[:, j * LANES:(j + 1) * LANES].T.astype(BF16)
    z_ref[...] = _dot(xb, wz_ref[...])
    xbc_ref[...] = _dot(xb, wxbc_ref[...])
    dt_ref[...] = _dot(xb, wdt_ref[...])


def _inproj(x2d, w, tabs, *, tm, nbatch, attn_copies):
    T = x2d.shape[0]
    nt = T // tm
    ntab = tabs[0].shape[0] // tm
    per_b = nt // nbatch
    full = lambda shape: pl.BlockSpec(shape, lambda i: (0,) * len(shape))
    row = lambda width: pl.BlockSpec((tm, width), lambda i: (i, 0))
    tab = pl.BlockSpec((tm, LANES), lambda i: (i % ntab, 0))
    in_specs = [row(D_MODEL), full((D_MODEL, 1024)), full((D_MODEL, 1024)), full((D_MODEL, 1024)),
                full((D_MODEL, 1024)), full((D_MODEL, CONV_DIM)), full((D_MODEL, LANES)), tab, tab, tab]
    k_shape, k_spec = jax.ShapeDtypeStruct((T, 1024), F32), row(1024)
    if attn_copies:
        k_shape = jax.ShapeDtypeStruct((nbatch, 1024, T // nbatch), F32)
        k_spec = pl.BlockSpec((1, 1024, tm), lambda i: (i // per_b, 0, i % per_b))
    out_shape = [jax.ShapeDtypeStruct((T, 1024), BF16), k_shape,
                 jax.ShapeDtypeStruct((T, 1024), F32), jax.ShapeDtypeStruct((T, 1024), F32),
                 jax.ShapeDtypeStruct((T, CONV_DIM), F32), jax.ShapeDtypeStruct((T, LANES), F32)]
    out_specs = [row(1024), k_spec, row(1024), row(1024), row(CONV_DIM), row(LANES)]
    if attn_copies:
        out_shape += [jax.ShapeDtypeStruct((T, 1024), BF16),
                      jax.ShapeDtypeStruct((nbatch, per_b, 1024, tm), BF16)]
        out_specs += [row(1024), pl.BlockSpec((1, 1, 1024, tm), lambda i: (i // per_b, i % per_b, 0, 0))]
    return pl.pallas_call(
        functools.partial(_inproj_body, attn_copies=attn_copies),
        grid=(nt,), in_specs=in_specs, out_specs=out_specs, out_shape=out_shape,
        compiler_params=_cparams(("parallel",)), name="inproj",
    )(x2d, w["wq"], w["wk"], w["wv"], w["wz"], w["wxbc"], w["wdt"], *tabs)


def _ssd_chunk_math(xpad_ref, dt_raw, lmat, mtot, e64, e128, convw, convb, dtb, alog):
    acc = convb
    for j in range(CONV_W):
        acc = acc + xpad_ref[pl.ds(8 - (CONV_W - 1) + j, CHUNK), :] * convw[j:j + 1, :]
    act = _silu(acc)
    xs = act[:, :SSD_INNER]
    lane = lax.broadcasted_iota(jnp.int32, (1, LANES), 1)
    a_row = jnp.where(lane < SSD_HEADS, -jnp.exp(alog), 0.0)
    dtv = _softplus(dt_raw + dtb)
    dA = dtv * a_row
    cs = _dot_sel_lhs(lmat, dA)
    cst = _dot_sel_lhs(mtot, dA)
    csT = cs.T
    ex = _dot_sel_rhs(jnp.concatenate([dtv, cs, cst], axis=0), e64)
    dt_e = ex[0:CHUNK]
    cs_e = ex[CHUNK:2 * CHUNK]
    cst_e = ex[2 * CHUNK:3 * CHUNK]
    cs_b = _dot_sel_rhs(cs, e128)
    X = xs * dt_e
    Xd = X * jnp.exp(cst_e - cs_e)
    ecs_e = jnp.exp(cs_e)
    lbool = lmat[:, 0:CHUNK].astype(F32) > 0.5
    lane2 = lax.broadcasted_iota(jnp.int32, (CHUNK, LANES), 1)
    ydiag = []
    bgs, cgs = [], []
    for g in range(SSD_GROUPS):
        bg = act[:, SSD_INNER + g * D_STATE:SSD_INNER + (g + 1) * D_STATE].astype(BF16)
        cg = act[:, SSD_INNER + (SSD_GROUPS + g) * D_STATE:SSD_INNER + (SSD_GROUPS + g + 1) * D_STATE].astype(BF16)
        bgs.append(bg)
        cgs.append(cg)
        cb = _dot_nt(cg, bg)
        for jj in range(SSD_HEADS // SSD_GROUPS // 2):
            h0 = g * (SSD_HEADS // SSD_GROUPS) + 2 * jj
            xp = X[:, h0 * SSD_HD:h0 * SSD_HD + LANES].astype(BF16)
            yp = []
            for h in (h0, h0 + 1):
                seg = cs_b[:, h * LANES:(h + 1) * LANES] - csT[h:h + 1, :]
                lm = jnp.exp(jnp.where(lbool, seg, NEG))
                yp.append(_dot((cb * lm).astype(BF16), xp))
            ydiag.append(jnp.where(lane2 < SSD_HD, yp[0], yp[1]))
    return dict(act=act, xs=xs, Xd=Xd, ecs_e=ecs_e, csT=csT, ydiag=ydiag, bgs=bgs, cgs=cgs)


def _ssd_finish(y_cols, xs, z, dskip, normw, out_ref):
    gz = []
    for j in range(SSD_INNER // LANES):
        sl = slice(j * LANES, (j + 1) * LANES)
        y = y_cols[j] + dskip[:, sl] * xs[:, sl]
        gz.append(y * _silu(z[:, sl]))
    per_g = SSD_INNER // SSD_GROUPS // LANES
    for g in range(SSD_GROUPS):
        blk = gz[g * per_g:(g + 1) * per_g]
        ss = sum(jnp.sum(b * b, axis=1, keepdims=True) for b in blk)
        inv = lax.rsqrt(ss / (SSD_INNER // SSD_GROUPS) + EPS)
        for j, b in enumerate(blk):
            sl = slice((g * per_g + j) * LANES, (g * per_g + j + 1) * LANES)
            out_ref[:, sl] = (b * inv * normw[:, sl]).astype(out_ref.dtype)


def _ssd_prompt_body(xbc_ref, dt_ref, z_ref, lmat_ref, mtot_ref, e64_ref, e128_ref, convw_ref, convb_ref,
                     dtb_ref, alog_ref, dskip_ref, normw_ref, out_ref, state_ref, xpad_ref):
    c = pl.program_id(1)

    @pl.when(c == 0)
    def _():
        xpad_ref[pl.ds(0, 8), :] = jnp.zeros((8, CONV_DIM), F32)
        state_ref[...] = jnp.zeros(state_ref.shape, F32)

    xpad_ref[pl.ds(8, CHUNK), :] = xbc_ref[...]
    m = _ssd_chunk_math(xpad_ref, dt_ref[...], lmat_ref[...], mtot_ref[...], e64_ref[...], e128_ref[...],
                        convw_ref[...], convb_ref[...], dtb_ref[...], alog_ref[...])
    xpad_ref[pl.ds(8 - (CONV_W - 1), CONV_W - 1), :] = xpad_ref[pl.ds(8 + CHUNK - (CONV_W - 1), CONV_W - 1), :]

    csT = m["csT"]
    cd = jnp.exp(jnp.broadcast_to(csT[0:SSD_HEADS, CHUNK - 1:CHUNK], (SSD_HEADS, LANES)))
    hpg = SSD_HEADS // SSD_GROUPS
    gw = hpg * SSD_HD
    y_cols = []
    for g in range(SSD_GROUPS):
        sg = state_ref[0, pl.ds(g * gw, gw), :]
        yoff = _dot_nt(m["cgs"][g], sg.astype(BF16))
        for jj in range(gw // LANES):
            col = g * (gw // LANES) + jj
            y_cols.append(m["ydiag"][col] + yoff[:, jj * LANES:(jj + 1) * LANES]
                          * m["ecs_e"][:, col * LANES:(col + 1) * LANES])
        xdg_t = m["Xd"][:, g * gw:(g + 1) * gw].T.astype(BF16)
        contrib = _dot(xdg_t, m["bgs"][g])
        for hh in range(hpg):
            h = g * hpg + hh
            rows = pl.ds(h * SSD_HD, SSD_HD)
            state_ref[0, rows, :] = (state_ref[0, rows, :] * cd[h:h + 1, :]
                                     + contrib[hh * SSD_HD:(hh + 1) * SSD_HD, :])
    _ssd_finish(y_cols, m["xs"], z_ref[...], dskip_ref[...], normw_ref[...], out_ref)


def _ssd_consts(p):
    return [p["e64"], p["e128"], p["conv_w"], p["conv_b"], p["dt_bias"], p["a_log"], p["d_skip"], p["norm_w"]]


def _const_specs(nd_grid):
    z = (0, 0)
    if nd_grid == 2:
        f = lambda shape: pl.BlockSpec(shape, lambda b, c: z)
    else:
        f = lambda shape: pl.BlockSpec(shape, lambda i: z)
    return [f((3 * LANES, SSD_INNER)), f((3 * LANES, SSD_HEADS * LANES)), f((CONV_W, CONV_DIM)), f((1, CONV_DIM)),
            f((1, LANES)), f((1, LANES)), f((1, SSD_INNER)), f((1, SSD_INNER))]


def _ssd_prompt(xbc, dt, z, p, nbatch):
    T = xbc.shape[0]
    nc = T // nbatch // CHUNK
    row = lambda width: pl.BlockSpec((CHUNK, width), lambda b, c: (b * nc + c, 0))
    sq = pl.BlockSpec((CHUNK, 3 * CHUNK), lambda b, c: (0, 0))
    tri = jnp.tile(jnp.tril(jnp.ones((CHUNK, CHUNK), F32)).astype(BF16), (1, 3))
    ones = jnp.ones((CHUNK, 3 * CHUNK), BF16)
    return pl.pallas_call(
        _ssd_prompt_body,
        grid=(nbatch, nc),
        in_specs=[row(CONV_DIM), row(LANES), row(SSD_INNER), sq, sq] + _const_specs(2),
        out_specs=[row(SSD_INNER), pl.BlockSpec((1, SSD_HEADS * SSD_HD, D_STATE), lambda b, c: (b, 0, 0))],
        out_shape=[jax.ShapeDtypeStruct((T, SSD_INNER), BF16),
                   jax.ShapeDtypeStruct((nbatch, SSD_HEADS * SSD_HD, D_STATE), F32)],
        scratch_shapes=[pltpu.VMEM((8 + CHUNK, CONV_DIM), F32)],
        compiler_params=_cparams(("parallel", "arbitrary")), name="ssd_prompt",
    )(xbc, dt, z, tri, ones, *_ssd_consts(p))


SROWS = 8
SB = CHUNK // SROWS


def _ssd_sample_body(xbc_ref, dt_ref, z_ref, s0_ref, lmat_ref, mtot_ref, e64_ref, e128_ref, convw_ref, convb_ref,
                     dtb_ref, alog_ref, dskip_ref, normw_ref, out_ref, s1_ref, xpad_ref):
    xpad_ref[pl.ds(0, 8), :] = jnp.zeros((8, CONV_DIM), F32)
    xpad_ref[pl.ds(8, CHUNK), :] = xbc_ref[...]
    m = _ssd_chunk_math(xpad_ref, dt_ref[...], lmat_ref[...], mtot_ref[...], e64_ref[...], e128_ref[...],
                        convw_ref[...], convb_ref[...], dtb_ref[...], alog_ref[...])
    csT = m["csT"]
    hpg = SSD_HEADS // SSD_GROUPS
    gw = hpg * SSD_HD
    rowi = lax.broadcasted_iota(jnp.int32, (CHUNK, LANES), 0)
    last_tok = CONV_W - 1 + DEC_Q - 1
    xdT = [m["Xd"][:, g * gw:(g + 1) * gw].T.astype(BF16) for g in range(SSD_GROUPS)]
    yoff = [[jnp.zeros((CHUNK, LANES), F32) for _ in range(gw // LANES)] for _ in range(SSD_GROUPS)]
    for b in range(SB):
        lo = b * SROWS + (CONV_W - 1)
        tok = (rowi >= lo) & (rowi <= b * SROWS + last_tok)
        col = b * SROWS + last_tok
        cd = jnp.exp(jnp.broadcast_to(csT[0:SSD_HEADS, col:col + 1], (SSD_HEADS, LANES)))
        for g in range(SSD_GROUPS):
            sg = s0_ref[b, pl.ds(g * gw, gw), :]
            res = _dot_nt(m["cgs"][g], sg.astype(BF16))
            for jj in range(gw // LANES):
                yoff[g][jj] = jnp.where(tok, res[:, jj * LANES:(jj + 1) * LANES], yoff[g][jj])
            bmask = jnp.where(tok, m["bgs"][g], jnp.zeros_like(m["bgs"][g]))
            contrib = _dot(xdT[g], bmask)
            for hh in range(hpg):
                h = g * hpg + hh
                rows = pl.ds(h * SSD_HD, SSD_HD)
                s1_ref[b, rows, :] = (s0_ref[b, rows, :] * cd[h:h + 1, :]
                                      + contrib[hh * SSD_HD:(hh + 1) * SSD_HD, :])
    y_cols = []
    for g in range(SSD_GROUPS):
        for jj in range(gw // LANES):
            col = g * (gw // LANES) + jj
            y_cols.append(m["ydiag"][col] + yoff[g][jj] * m["ecs_e"][:, col * LANES:(col + 1) * LANES])
    _ssd_finish(y_cols, m["xs"], z_ref[...], dskip_ref[...], normw_ref[...], out_ref)


def _ssd_sample(xbc8, dt8, z8, s0, p):
    T8 = xbc8.shape[0]
    nblk = T8 // CHUNK
    row = lambda width: pl.BlockSpec((CHUNK, width), lambda i: (i, 0))
    sq = pl.BlockSpec((CHUNK, 3 * CHUNK), lambda i: (0, 0))
    st = pl.BlockSpec((SB, SSD_HEADS * SSD_HD, D_STATE), lambda i: (i, 0, 0))
    r = jnp.arange(CHUNK)
    same = (r[:, None] // SROWS) == (r[None, :] // SROWS)
    is_tok = ((r % SROWS) >= CONV_W - 1) & ((r % SROWS) < SROWS - 1)
    lmat = jnp.tile((same & (r[None, :] <= r[:, None]) & is_tok[None, :] & is_tok[:, None]).astype(BF16), (1, 3))
    mtot = jnp.tile((same & is_tok[None, :]).astype(BF16), (1, 3))
    return pl.pallas_call(
        _ssd_sample_body,
        grid=(nblk,),
        in_specs=[row(CONV_DIM), row(LANES), row(SSD_INNER), st, sq, sq] + _const_specs(1),
        out_specs=[row(SSD_INNER), st],
        out_shape=[jax.ShapeDtypeStruct((T8, SSD_INNER), BF16), jax.ShapeDtypeStruct(s0.shape, F32)],
        scratch_shapes=[pltpu.VMEM((8 + CHUNK, CONV_DIM), F32)],
        compiler_params=_cparams(("parallel",)), name="ssd_sample",
    )(xbc8, dt8, z8, s0, lmat, mtot, *_ssd_consts(p))


def _lam_value(lq1, lk1, lq2, lk2, lam_init):
    s1 = jnp.sum(lq1[...] * lk1[...], axis=1, keepdims=True)
    s2 = jnp.sum(lq2[...] * lk2[...], axis=1, keepdims=True)
    return jnp.exp(s1) - jnp.exp(s2) + lam_init


def _flash_body(q_ref, k_ref, vt_ref, lq1, lk1, lq2, lk2, sw_ref, o_ref, sa_ref, sb_ref, bma_ref, bmb_ref,
                m_ref, acc_ref, *, tq, tk, lam_init):
    i = pl.program_id(2)
    qt = q_ref[...].astype(F32).T
    row = lax.broadcasted_iota(jnp.int32, (LANES, tq), 0)
    qst = jnp.concatenate([jnp.where(row < ATT_HD, qt, 0.0), jnp.where(row >= ATT_HD, qt, 0.0)],
                          axis=1).astype(BF16)
    ones = jnp.ones((ONES_ROWS, tk), BF16)
    m_ref[...] = jnp.full(m_ref.shape, NEG, F32)
    acc_ref[...] = jnp.zeros(acc_ref.shape, F32)

    def produce(blk, s_ref, bm_ref):
        kblk = k_ref[pl.ds(pl.multiple_of(blk * tk, tk), tk), :]
        s = _dot(kblk, qst)
        s_ref[...] = s
        bm_ref[...] = jnp.max(s, axis=0, keepdims=True)

    def consume(blk, s_ref, bm_ref, off):
        s = s_ref[...]
        if off is None:
            bmax = bm_ref[...]
        else:
            kpos = lax.broadcasted_iota(jnp.int32, (tk, 2 * tq), 0) + off
            qpos = lax.broadcasted_iota(jnp.int32, (tk, 2 * tq), 1)
            qpos = jnp.where(qpos >= tq, qpos - tq, qpos)
            s = jnp.where(kpos <= qpos, s, NEG)
            bmax = jnp.max(s, axis=0, keepdims=True)
        m_old = m_ref[...]
        m_new = jnp.maximum(m_old, bmax)
        alpha = jnp.exp2(m_old - m_new)
        p = jnp.exp2((s - m_new).astype(BF16))
        m_ref[...] = m_new
        vt1 = jnp.concatenate([vt_ref[0, blk], ones], axis=0)
        acc_ref[...] = alpha * acc_ref[...] + _dot(vt1, p)

    produce(0, sa_ref, bma_ref)

    def pair(p):
        produce(2 * p + 1, sb_ref, bmb_ref)
        consume(2 * p, sa_ref, bma_ref, None)
        produce(2 * p + 2, sa_ref, bma_ref)
        consume(2 * p + 1, sb_ref, bmb_ref, None)

    def body(pp, carry):
        pair(2 * pp)
        pair(2 * pp + 1)
        return carry

    lax.fori_loop(0, i // 2, body, 0)

    @pl.when(i % 2 == 1)
    def _():
        pair(i - 1)

    produce(2 * i + 1, sb_ref, bmb_ref)
    consume(2 * i, sa_ref, bma_ref, 0)
    consume(2 * i + 1, sb_ref, bmb_ref, tk)

    lam = _lam_value(lq1, lk1, lq2, lk2, lam_init)
    acc = acc_ref[0:LANES, :]
    l = acc_ref[LANES:LANES + 1, :]
    o = acc[:, :tq] / l[:, :tq] - lam * (acc[:, tq:] / l[:, tq:])
    inv = lax.rsqrt(jnp.mean(o * o, axis=0, keepdims=True) + EPS)
    o_ref[...] = ((o * inv).T * (sw_ref[...] * (1.0 - lam_init))).astype(o_ref.dtype)


def _flash(q, kb, vt, lams, subln, *, lam_init):
    T = q.shape[0]
    nb, nk, _, tk = vt.shape
    tq = 2 * tk
    L = T // nb
    nq = L // tq
    vec = pl.BlockSpec((1, ATT_HD), lambda b, h, i: (0, 0))
    return pl.pallas_call(
        functools.partial(_flash_body, tq=tq, tk=tk, lam_init=lam_init),
        grid=(nb, ATT_HEADS, nq),
        in_specs=[pl.BlockSpec((tq, LANES), lambda b, h, i: (b * nq + i, h)),
                  pl.BlockSpec((L, LANES), lambda b, h, i: (b, h)),
                  pl.BlockSpec((1, nk, LANES, tk), lambda b, h, i: (b, 0, h, 0)),
                  vec, vec, vec, vec, pl.BlockSpec((1, LANES), lambda b, h, i: (0, 0))],
        out_specs=pl.BlockSpec((tq, LANES), lambda b, h, i: (b * nq + i, h)),
        out_shape=jax.ShapeDtypeStruct((T, ATT_WIDTH), BF16),
        scratch_shapes=[pltpu.VMEM((tk, 2 * tq), F32), pltpu.VMEM((tk, 2 * tq), F32),
                        pltpu.VMEM((1, 2 * tq), F32), pltpu.VMEM((1, 2 * tq), F32),
                        pltpu.VMEM((1, 2 * tq), F32), pltpu.VMEM((LANES + ONES_ROWS, 2 * tq), F32)],
        compiler_params=_cparams(("parallel", "parallel", "arbitrary")), name="flash_diff",
    )(q, kb, vt, *lams, subln)


ONES_ROWS = 16
PAGES_PER_STEP = 16
DEC_Q = 4


def _decode_body(pt_ref, qbd_ref, kn_ref, vx_ref, *refs, pps, page, lam_init):
    k_refs = refs[:pps]
    v_refs = refs[pps:2 * pps]
    lq1, lk1, lq2, lk2, sw_ref, o_ref, m_ref, l_ref, acc_ref = refs[2 * pps:]
    j = pl.program_id(1)
    nsteps = pl.num_programs(1)
    rows_per_head = 2 * DEC_Q

    @pl.when(j == 0)
    def _():
        m_ref[...] = jnp.full(m_ref.shape, NEG, F32)
        l_ref[...] = jnp.zeros(l_ref.shape, F32)
        acc_ref[...] = jnp.zeros(acc_ref.shape, F32)

    qbd = qbd_ref[0]
    kk = jnp.concatenate([k_refs[i][0].reshape(2 * ATT_HEADS * ATT_HD, page).astype(BF16) for i in range(pps)],
                         axis=1)
    s = _dot(qbd, kk)
    m_old = m_ref[...]
    m_new = jnp.maximum(m_old, jnp.max(s, axis=1, keepdims=True))
    alpha = jnp.exp2(m_old - m_new)
    p = jnp.exp2(s - m_new)
    l_ref[...] = alpha * l_ref[...] + jnp.sum(p, axis=1, keepdims=True)
    m_ref[...] = m_new
    pb = p.astype(BF16)

    def head_values(h):
        return jnp.concatenate([v_refs[i][0, pl.ds(h, page, stride=ATT_HEADS), :] for i in range(pps)],
                               axis=0).astype(BF16)

    for h in range(0, ATT_HEADS, 2):
        vv = jnp.concatenate([head_values(h), head_values(h + 1)], axis=1)
        r0 = h * rows_per_head
        pv = _dot(pb[r0:r0 + 2 * rows_per_head, :], vv)
        for u in range(2):
            rows = pl.ds(r0 + u * rows_per_head, rows_per_head)
            acc_ref[rows, :] = (alpha[r0 + u * rows_per_head:r0 + (u + 1) * rows_per_head, :] * acc_ref[rows, :]
                                + pv[u * rows_per_head:(u + 1) * rows_per_head, u * LANES:(u + 1) * LANES])

    @pl.when(j == nsteps - 1)
    def _():
        hist = CONV_W - 1
        sn = _dot_nt(qbd, kn_ref[0].astype(BF16))
        r = lax.broadcasted_iota(jnp.int32, sn.shape, 0)
        t = lax.broadcasted_iota(jnp.int32, sn.shape, 1) - hist
        sn = jnp.where((t >= 0) & (t < DEC_Q) & (t <= r % DEC_Q), sn, NEG)
        m_o = m_ref[...]
        m_n = jnp.maximum(m_o, jnp.max(sn, axis=1, keepdims=True))
        a = jnp.exp2(m_o - m_n)
        pn = jnp.exp2(sn - m_n)
        l = a * l_ref[...] + jnp.sum(pn, axis=1, keepdims=True)
        acc = a * acc_ref[...]
        for tt in range(DEC_Q):
            acc = acc + pn[:, hist + tt:hist + tt + 1] * vx_ref[0, tt]
        lam = _lam_value(lq1, lk1, lq2, lk2, lam_init)
        o1 = acc / l
        nrow = o1.shape[0]
        o = o1 - lam * pltpu.roll(o1, nrow - DEC_Q, 0)
        inv = lax.rsqrt(jnp.mean(o * o, axis=1, keepdims=True) + EPS)
        o_ref[0] = (o * inv * (sw_ref[...] * (1.0 - lam_init))).astype(o_ref.dtype)


def _decode_attn(qbd, knew, vexp, ckt, cv, page_table, lams, subln, *, lam_init):
    nb, npages = page_table.shape
    page = ckt.shape[3]
    pps = PAGES_PER_STEP
    nsteps = npages // pps
    nrow = 2 * ATT_HEADS * DEC_Q

    def kspec(i):
        return pl.BlockSpec((1, 2 * ATT_HEADS, ATT_HD, page), lambda b, j, pt: (pt[b, j * pps + i], 0, 0, 0))

    def vspec(i):
        return pl.BlockSpec((1, page * ATT_HEADS, LANES), lambda b, j, pt: (pt[b, j * pps + i], 0, 0))

    vec = pl.BlockSpec((1, ATT_HD), lambda b, j, pt: (0, 0))
    in_specs = ([pl.BlockSpec((1, nrow, 2 * ATT_HEADS * ATT_HD), lambda b, j, pt: (b, 0, 0)),
                 pl.BlockSpec((1, SROWS, 2 * ATT_HEADS * ATT_HD), lambda b, j, pt: (b, 0, 0)),
                 pl.BlockSpec((1, DEC_Q, nrow, LANES), lambda b, j, pt: (b, 0, 0, 0))]
                + [kspec(i) for i in range(pps)] + [vspec(i) for i in range(pps)]
                + [vec, vec, vec, vec, pl.BlockSpec((1, LANES), lambda b, j, pt: (0, 0))])
    grid_spec = pltpu.PrefetchScalarGridSpec(
        num_scalar_prefetch=1, grid=(nb, nsteps), in_specs=in_specs,
        out_specs=pl.BlockSpec((1, nrow, LANES), lambda b, j, pt: (b, 0, 0)),
        scratch_shapes=[pltpu.VMEM((nrow, 1), F32), pltpu.VMEM((nrow, 1), F32), pltpu.VMEM((nrow, LANES), F32)])
    return pl.pallas_call(
        functools.partial(_decode_body, pps=pps, page=page, lam_init=lam_init),
        grid_spec=grid_spec,
        out_shape=jax.ShapeDtypeStruct((nb, nrow, LANES), BF16),
        compiler_params=_cparams(("parallel", "arbitrary")), name="decode_attn",
    )(page_table, qbd, knew, vexp, *([ckt] * pps), *([cv] * pps), *lams, subln)


def _layer_norm(x, g, b):
    mu = jnp.mean(x, axis=1, keepdims=True)
    xc = x - mu
    var = jnp.mean(xc * xc, axis=1, keepdims=True)
    return xc * lax.rsqrt(var + EPS) * g + b


def _post_body(att_ref, ssd_ref, x_ref, pe_ref, woa_ref, wos_ref, wg_ref, wu_ref, wfo_ref, wpe_ref, wpg_ref,
               g1_ref, b1_ref, g2_ref, b2_ref, o_ref, *, alpha, nff):
    mix = _dot(att_ref[...], woa_ref[...]) + _dot(ssd_ref[...], wos_ref[...])
    h = _layer_norm(alpha * x_ref[...] + mix, g1_ref[...], b1_ref[...])
    hb = h.astype(BF16)
    ffn = None
    step = D_FF // nff
    for c in range(nff):
        sl = slice(c * step, (c + 1) * step)
        gate = _dot(hb, wg_ref[:, sl])
        up = _dot(hb, wu_ref[:, sl])
        part = _dot((_silu(gate) * up).astype(BF16), wfo_ref[sl, :])
        ffn = part if ffn is None else ffn + part
    pemb = _dot(pe_ref[...].astype(BF16), wpe_ref[...]) * jax.nn.sigmoid(_dot(hb, wpg_ref[...]))
    o_ref[...] = _layer_norm(alpha * h + ffn + pemb, g2_ref[...], b2_ref[...])


def _post(att, ssd, x2d, pe2d, w, *, tm, alpha):
    T = x2d.shape[0]
    row = lambda width: pl.BlockSpec((tm, width), lambda i: (i, 0))
    full = lambda shape: pl.BlockSpec(shape, lambda i: (0, 0), pipeline_mode=pl.Buffered(1))
    return pl.pallas_call(
        functools.partial(_post_body, alpha=alpha, nff=2),
        grid=(T // tm,),
        in_specs=[row(1024), row(1024), row(D_MODEL), row(P_DIM),
                  full((1024, D_MODEL)), full((1024, D_MODEL)), full((D_MODEL, D_FF)), full((D_MODEL, D_FF)),
                  full((D_FF, D_MODEL)), full((P_DIM, D_MODEL)), full((D_MODEL, D_MODEL)),
                  full((1, D_MODEL)), full((1, D_MODEL)), full((1, D_MODEL)), full((1, D_MODEL))],
        out_specs=row(D_MODEL),
        out_shape=jax.ShapeDtypeStruct((T, D_MODEL), F32),
        compiler_params=_cparams(("parallel",)), name="post",
    )(att, ssd, x2d, pe2d, w["woa"], w["wos"], w["wg"], w["wu"], w["wfo"], w["wpe"], w["wpg"],
      w["g1"], w["b1"], w["g2"], w["b2"])


def _rope_tables(pos):
    half = ROT_DIM // 2
    inv = ROPE_THETA ** (-jnp.arange(half, dtype=F32) * 2.0 / ROT_DIM)
    ang = pos.astype(F32)[:, None] * inv[None, :]
    cos, sin = jnp.cos(ang), jnp.sin(ang)
    n = pos.shape[0]
    pad = jnp.zeros((n, ATT_HD - ROT_DIM), F32)
    ra = jnp.concatenate([cos, cos, jnp.ones((n, ATT_HD - ROT_DIM), F32)], 1)
    rb = jnp.concatenate([-sin, jnp.zeros((n, half), F32), pad], 1)
    rc = jnp.concatenate([jnp.zeros((n, half), F32), sin, pad], 1)
    rep = LANES // ATT_HD
    return tuple(jnp.tile(t, (1, rep)) for t in (ra, rb, rc))


def _layer(depth, lam_init, x_prompt, x_sample, p_prompt, p_sample, cache_k, cache_v, page_table, state_ssm,
           state_conv, prm):
    (w_in, conv_w, conv_b, dt_bias, a_log, d_skip, ssd_norm_w, lq1, lk1, lq2, lk2, subln_w,
     w_out, ln1_g, ln1_b, w_ffn_in, w_ffn_out, w_pe, w_pg, ln2_g, ln2_b) = prm
    bp, lp, _ = x_prompt.shape
    bs, ls, _ = x_sample.shape
    npages = page_table.shape[1]
    page = cache_k.shape[1]
    past_len = npages * page
    alpha = (2 * depth) ** 0.25

    wb = w_in.astype(BF16)
    wq = wb[:, 0:1024]
    wk = wb[:, 1024:2048]
    wv = wb[:, 2048:3072]
    wz = wb[:, 3072:4096]
    wxbc = wb[:, 4096:4096 + CONV_DIM]
    wdt = jnp.pad(wb[:, 4096 + CONV_DIM:], ((0, 0), (0, LANES - SSD_HEADS)))
    w1 = dict(wq=wq, wk=wk, wv=wv, wz=wz, wxbc=wxbc, wdt=wdt)

    hrep = jnp.arange(SSD_HEADS * SSD_HD) // SSD_HD
    e64 = (jnp.arange(LANES)[:, None] == hrep[None, :]).astype(BF16)
    e128 = (jnp.arange(LANES)[:, None] == (jnp.arange(SSD_HEADS * LANES) // LANES)[None, :]).astype(BF16)
    padl = lambda v: jnp.pad(v.astype(F32), (0, LANES - v.shape[0]))[None, :]
    pssd = dict(e64=jnp.tile(e64, (3, 1)), e128=jnp.tile(e128, (3, 1)), conv_w=conv_w.astype(F32), conv_b=conv_b.astype(F32)[None, :],
                dt_bias=padl(dt_bias), a_log=padl(a_log), d_skip=jnp.repeat(d_skip.astype(F32), SSD_HD)[None, :],
                norm_w=ssd_norm_w.astype(F32)[None, :])
    lams = [v.astype(F32)[None, :] for v in (lq1, lk1, lq2, lk2)]
    subln = subln_w.astype(F32)[None, :]
    wob = w_out.astype(BF16)
    wfi = w_ffn_in.astype(BF16)
    w4 = dict(woa=wob[:ATT_WIDTH], wos=wob[ATT_WIDTH:], wg=wfi[:, :D_FF], wu=wfi[:, D_FF:],
              wfo=w_ffn_out.astype(BF16), wpe=w_pe.astype(BF16), wpg=w_pg.astype(BF16),
              g1=ln1_g.astype(F32)[None, :], b1=ln1_b.astype(F32)[None, :],
              g2=ln2_g.astype(F32)[None, :], b2=ln2_b.astype(F32)[None, :])

    tm = 256
    xp2 = x_prompt.reshape(bp * lp, D_MODEL)
    q, kt, vf, z, xbc, dt, kb, vt = _inproj(xp2, w1, _rope_tables(jnp.arange(lp)), tm=tm, nbatch=bp,
                                            attn_copies=True)
    k_p = jnp.transpose(kt.reshape(bp, 2 * ATT_HEADS, ATT_HD, lp), (0, 3, 1, 2))
    ssd_p, ssm_p = _ssd_prompt(xbc, dt, z, pssd, bp)
    att_p = _flash(q, kb, vt, lams, subln, lam_init=lam_init)
    y_p = _post(att_p, ssd_p, xp2, p_prompt.reshape(bp * lp, P_DIM), w4, tm=2 * tm, alpha=alpha)
    conv_p = xbc.reshape(bp, lp, CONV_DIM)[:, lp - (CONV_W - 1):, :]

    hist = CONV_W - 1
    x8 = jnp.zeros((bs, SROWS, D_MODEL), F32).at[:, hist:hist + ls].set(x_sample)
    pos8 = jnp.tile(jnp.clip(jnp.arange(SROWS) - hist, 0, ls - 1) + past_len, bs)
    qs, kfs, vfs, zs, xbcs, dts = _inproj(x8.reshape(bs * SROWS, D_MODEL), w1, _rope_tables(pos8),
                                          tm=256, nbatch=1, attn_copies=False)
    xbc8 = xbcs.reshape(bs, SROWS, CONV_DIM).at[:, :hist].set(state_conv.astype(F32))
    ssd_s8, ssm_s = _ssd_sample(xbc8.reshape(bs * SROWS, CONV_DIM), dts, zs,
                                state_ssm.reshape(bs, SSD_HEADS * SSD_HD, D_STATE), pssd)
    tok = lambda a, w: a.reshape(bs, SROWS, w)[:, hist:hist + ls]
    ncomp = 2 * ATT_HEADS
    q4 = tok(qs, 1024).reshape(bs, ls, ncomp, ATT_HD)
    qcq = jnp.transpose(q4, (0, 2, 1, 3)).reshape(bs, ncomp * ls, ATT_HD)
    slot = (jnp.arange(ncomp * ls)[:, None] // ls) == (jnp.arange(ncomp * ATT_HD)[None, :] // ATT_HD)
    qbd = jnp.where(slot[None], jnp.tile(qcq, (1, 1, ncomp)), jnp.zeros((), BF16))
    k_s = tok(kfs, 1024)
    v_s = tok(vfs, 1024)
    vexp = jnp.repeat(v_s.reshape(bs, ls, ATT_HEADS, 2 * ATT_HD), 2 * ls, axis=2)
    att_s = _decode_attn(qbd, kfs.reshape(bs, SROWS, 1024), vexp,
                         jnp.transpose(cache_k, (0, 2, 3, 1)),
                         cache_v.reshape(cache_v.shape[0], page * ATT_HEADS, 2 * ATT_HD),
                         page_table, lams, subln, lam_init=lam_init)
    att_s = att_s.reshape(bs, ATT_HEADS, 2, ls, LANES)[:, :, 0]
    att_s = jnp.transpose(att_s, (0, 2, 1, 3)).reshape(bs * ls, ATT_WIDTH)
    ssd_s = tok(ssd_s8, SSD_INNER).reshape(bs * ls, SSD_INNER)
    y_s = _post(att_s, ssd_s, x_sample.reshape(bs * ls, D_MODEL), p_sample.reshape(bs * ls, P_DIM), w4,
                tm=min(256, bs * ls), alpha=alpha)
    conv_s = tok(xbcs, CONV_DIM)[:, ls - hist:]

    return (y_p.reshape(bp, lp, D_MODEL), y_s.reshape(bs, ls, D_MODEL),
            k_p, vf.reshape(bp, lp, ATT_HEADS, 2 * ATT_HD),
            ssm_p.reshape(bp, SSD_HEADS, SSD_HD, D_STATE), conv_p,
            k_s.reshape(bs, ls, 2 * ATT_HEADS, ATT_HD), v_s.reshape(bs, ls, ATT_HEADS, 2 * ATT_HD),
            ssm_s.reshape(bs, SSD_HEADS, SSD_HD, D_STATE), conv_s)


def kernel(x_prompt, x_sample, p_prompt, p_sample, cache_k, cache_v, page_table, state_ssm, state_conv, w_in, conv_w, conv_b, dt_bias, a_log, d_skip, ssd_norm_w, lambda_q1, lambda_k1, lambda_q2, lambda_k2, subln_w, w_out, ln1_g, ln1_b, w_ffn_in, w_ffn_out, w_pe, w_pg, ln2_g, ln2_b):
    depth = w_in.shape[0]
    assert depth == 1, "single-layer trunk"
    prm = tuple(a[0] for a in (w_in, conv_w, conv_b, dt_bias, a_log, d_skip, ssd_norm_w, lambda_q1, lambda_k1,
                               lambda_q2, lambda_k2, subln_w, w_out, ln1_g, ln1_b, w_ffn_in, w_ffn_out, w_pe,
                               w_pg, ln2_g, ln2_b))
    lam_init = 0.8 - 0.6 * math.exp(-0.3 * 0)
    outs = _layer(depth, lam_init, x_prompt, x_sample, p_prompt[0], p_sample[0], cache_k[0], cache_v[0], page_table,
                  state_ssm[0], state_conv[0], prm)
    y_p, y_s = outs[0], outs[1]
    return (y_p, y_s) + tuple(o[None] for o in outs[2:])
```

```python
import functools
import math

import jax
import jax.numpy as jnp
from jax import lax
from jax.experimental import pallas as pl
from jax.experimental.pallas import tpu as pltpu

F32 = jnp.float32
BF16 = jnp.bfloat16

D_MODEL = 1024
ATT_HD = 64
ATT_HEADS = 8
ATT_WIDTH = 1024
ROT_DIM = 16
ROPE_THETA = 500000.0
SSD_HD = 64
SSD_INNER = 1024
SSD_HEADS = 16
SSD_GROUPS = 2
D_STATE = 128
CONV_W = 4
CONV_DIM = 1536
CHUNK = 128
D_FF = 2816
P_DIM = 256
EPS = 1e-5
LANES = 128
NEG = -1e30
Q_SCALE = (ATT_HD ** -0.5) * math.log2(math.e)

VMEM_LIMIT = 56 * 1024 * 1024


def _cparams(sem):
    return pltpu.CompilerParams(dimension_semantics=sem, vmem_limit_bytes=VMEM_LIMIT)


def _dot(a, b):
    return jnp.dot(a, b, preferred_element_type=F32)


def _dot_nt(a, b):
    return lax.dot_general(a, b, (((1,), (1,)), ((), ())), preferred_element_type=F32)


def _split3(a):
    hi = a.astype(BF16)
    r = a - hi.astype(F32)
    mid = r.astype(BF16)
    lo = (r - mid.astype(F32)).astype(BF16)
    return hi, mid, lo


def _dot_sel_lhs(sel3, a):
    hi, mid, lo = _split3(a)
    return _dot(sel3, jnp.concatenate([hi, mid, lo], axis=0))


def _dot_sel_rhs(a, sel3):
    hi, mid, lo = _split3(a)
    return _dot(jnp.concatenate([hi, mid, lo], axis=1), sel3)


def _silu(x):
    h = 0.5 * x
    return h + h * jnp.tanh(h)


def _softplus(x):
    return jnp.maximum(x, 0.0) + jnp.log1p(jnp.exp(-jnp.abs(x)))


def _rope_cols(t, ra, rb, rc):
    outs = []
    for j in range(t.shape[1] // LANES):
        c = t[:, j * LANES:(j + 1) * LANES]
        outs.append(c * ra + pltpu.roll(c, LANES - ROT_DIM // 2, 1) * rb + pltpu.roll(c, ROT_DIM // 2, 1) * rc)
    return outs


def _inproj_body(x_ref, wq_ref, wk_ref, wv_ref, wz_ref, wxbc_ref, wdt_ref, ra_ref, rb_ref, rc_ref,
                 q_ref, kf_ref, vf_ref, z_ref, xbc_ref, dt_ref, *maybe_bf16, attn_copies):
    xb = x_ref[...].astype(BF16)
    ra = ra_ref[...]
    rb = rb_ref[...]
    rc = rc_ref[...]
    q = _dot(xb, wq_ref[...])
    for j, c in enumerate(_rope_cols(q, ra, rb, rc)):
        q_ref[:, j * LANES:(j + 1) * LANES] = (c * Q_SCALE).astype(BF16)
    k = _dot(xb, wk_ref[...])
    for j, c in enumerate(_rope_cols(k, ra, rb, rc)):
        if attn_copies:
            maybe_bf16[0][:, j * LANES:(j + 1) * LANES] = c.astype(BF16)
            kf_ref[0, j * LANES:(j + 1) * LANES, :] = c.T
        else:
            kf_ref[:, j * LANES:(j + 1) * LANES] = c
    v = _dot(xb, wv_ref[...])
    vf_ref[...] = v
    if attn_copies:
        for j in range(v.shape[1] // LANES):
            maybe_bf16[1][0, 0, j * LANES:(j + 1) * LANES, :] = v[:, j * LANES:(j + 1) * LANES].T.astype(BF16)
    z_ref[...] = _dot(xb, wz_ref[...])
    xbc_ref[...] = _dot(xb, wxbc_ref[...])
    dt_ref[...] = _dot(xb, wdt_ref[...])


def _inproj(x2d, w, tabs, *, tm, nbatch, attn_copies):
    T = x2d.shape[0]
    nt = T // tm
    ntab = tabs[0].shape[0] // tm
    per_b = nt // nbatch
    full = lambda shape: pl.BlockSpec(shape, lambda i: (0,) * len(shape))
    row = lambda width: pl.BlockSpec((tm, width), lambda i: (i, 0))
    tab = pl.BlockSpec((tm, LANES), lambda i: (i % ntab, 0))
    in_specs = [row(D_MODEL), full((D_MODEL, 1024)), full((D_MODEL, 1024)), full((D_MODEL, 1024)),
                full((D_MODEL, 1024)), full((D_MODEL, CONV_DIM)), full((D_MODEL, LANES)), tab, tab, tab]
    k_shape, k_spec = jax.ShapeDtypeStruct((T, 1024), F32), row(1024)
    if attn_copies:
        k_shape = jax.ShapeDtypeStruct((nbatch, 1024, T // nbatch), F32)
        k_spec = pl.BlockSpec((1, 1024, tm), lambda i: (i // per_b, 0, i % per_b))
    out_shape = [jax.ShapeDtypeStruct((T, 1024), BF16), k_shape,
                 jax.ShapeDtypeStruct((T, 1024), F32), jax.ShapeDtypeStruct((T, 1024), F32),
                 jax.ShapeDtypeStruct((T, CONV_DIM), F32), jax.ShapeDtypeStruct((T, LANES), F32)]
    out_specs = [row(1024), k_spec, row(1024), row(1024), row(CONV_DIM), row(LANES)]
    if attn_copies:
        out_shape += [jax.ShapeDtypeStruct((T, 1024), BF16),
                      jax.ShapeDtypeStruct((nbatch, per_b, 1024, tm), BF16)]
        out_specs += [row(1024), pl.BlockSpec((1, 1, 1024, tm), lambda i: (i // per_b, i % per_b, 0, 0))]
    return pl.pallas_call(
        functools.partial(_inproj_body, attn_copies=attn_copies),
        grid=(nt,), in_specs=in_specs, out_specs=out_specs, out_shape=out_shape,
        compiler_params=_cparams(("parallel",)), name="inproj",
    )(x2d, w["wq"], w["wk"], w["wv"], w["wz"], w["wxbc"], w["wdt"], *tabs)


def _ssd_chunk_math(xpad_ref, dt_raw, lmat, mtot, e64, e128, convw, convb, dtb, alog):
    acc = convb
    for j in range(CONV_W):
        acc = acc + xpad_ref[pl.ds(8 - (CONV_W - 1) + j, CHUNK), :] * convw[j:j + 1, :]
    act = _silu(acc)
    xs = act[:, :SSD_INNER]
    lane = lax.broadcasted_iota(jnp.int32, (1, LANES), 1)
    a_row = jnp.where(lane < SSD_HEADS, -jnp.exp(alog), 0.0)
    dtv = _softplus(dt_raw + dtb)
    dA = dtv * a_row
    cs = _dot_sel_lhs(lmat, dA)
    cst = _dot_sel_lhs(mtot, dA)
    csT = cs.T
    ex = _dot_sel_rhs(jnp.concatenate([dtv, cs, cst], axis=0), e64)
    dt_e = ex[0:CHUNK]
    cs_e = ex[CHUNK:2 * CHUNK]
    cst_e = ex[2 * CHUNK:3 * CHUNK]
    cs_b = _dot_sel_rhs(cs, e128)
    X = xs * dt_e
    Xd = X * jnp.exp(cst_e - cs_e)
    ecs_e = jnp.exp(cs_e)
    lbool = lmat[:, 0:CHUNK].astype(F32) > 0.5
    lane2 = lax.broadcasted_iota(jnp.int32, (CHUNK, LANES), 1)
    ydiag = []
    bgs, cgs = [], []
    for g in range(SSD_GROUPS):
        bg = act[:, SSD_INNER + g * D_STATE:SSD_INNER + (g + 1) * D_STATE].astype(BF16)
        cg = act[:, SSD_INNER + (SSD_GROUPS + g) * D_STATE:SSD_INNER + (SSD_GROUPS + g + 1) * D_STATE].astype(BF16)
        bgs.append(bg)
        cgs.append(cg)
        cb = _dot_nt(cg, bg)
        for jj in range(SSD_HEADS // SSD_GROUPS // 2):
            h0 = g * (SSD_HEADS // SSD_GROUPS) + 2 * jj
            xp = X[:, h0 * SSD_HD:h0 * SSD_HD + LANES].astype(BF16)
            yp = []
            for h in (h0, h0 + 1):
                seg = cs_b[:, h * LANES:(h + 1) * LANES] - csT[h:h + 1, :]
                lm = jnp.exp(jnp.where(lbool, seg, NEG))
                yp.append(_dot((cb * lm).astype(BF16), xp))
            ydiag.append(jnp.where(lane2 < SSD_HD, yp[0], yp[1]))
    return dict(act=act, xs=xs, Xd=Xd, ecs_e=ecs_e, csT=csT, ydiag=ydiag, bgs=bgs, cgs=cgs)


def _ssd_finish(y_cols, xs, z, dskip, normw, out_ref):
    gz = []
    for j in range(SSD_INNER // LANES):
        sl = slice(j * LANES, (j + 1) * LANES)
        y = y_cols[j] + dskip[:, sl] * xs[:, sl]
        gz.append(y * _silu(z[:, sl]))
    per_g = SSD_INNER // SSD_GROUPS // LANES
    for g in range(SSD_GROUPS):
        blk = gz[g * per_g:(g + 1) * per_g]
        ss = sum(jnp.sum(b * b, axis=1, keepdims=True) for b in blk)
        inv = lax.rsqrt(ss / (SSD_INNER // SSD_GROUPS) + EPS)
        for j, b in enumerate(blk):
            sl = slice((g * per_g + j) * LANES, (g * per_g + j + 1) * LANES)
            out_ref[:, sl] = (b * inv * normw[:, sl]).astype(out_ref.dtype)


def _ssd_prompt_body(xbc_ref, dt_ref, z_ref, lmat_ref, mtot_ref, e64_ref, e128_ref, convw_ref, convb_ref,
                     dtb_ref, alog_ref, dskip_ref, normw_ref, out_ref, state_ref, xpad_ref):
    c = pl.program_id(1)

    @pl.when(c == 0)
    def _():
        xpad_ref[pl.ds(0, 8), :] = jnp.zeros((8, CONV_DIM), F32)
        state_ref[...] = jnp.zeros(state_ref.shape, F32)

    xpad_ref[pl.ds(8, CHUNK), :] = xbc_ref[...]
    m = _ssd_chunk_math(xpad_ref, dt_ref[...], lmat_ref[...], mtot_ref[...], e64_ref[...], e128_ref[...],
                        convw_ref[...], convb_ref[...], dtb_ref[...], alog_ref[...])
    xpad_ref[pl.ds(8 - (CONV_W - 1), CONV_W - 1), :] = xpad_ref[pl.ds(8 + CHUNK - (CONV_W - 1), CONV_W - 1), :]

    csT = m["csT"]
    cd = jnp.exp(jnp.broadcast_to(csT[0:SSD_HEADS, CHUNK - 1:CHUNK], (SSD_HEADS, LANES)))
    hpg = SSD_HEADS // SSD_GROUPS
    gw = hpg * SSD_HD
    y_cols = []
    for g in range(SSD_GROUPS):
        sg = state_ref[0, pl.ds(g * gw, gw), :]
        yoff = _dot_nt(m["cgs"][g], sg.astype(BF16))
        for jj in range(gw // LANES):
            col = g * (gw // LANES) + jj
            y_cols.append(m["ydiag"][col] + yoff[:, jj * LANES:(jj + 1) * LANES]
                          * m["ecs_e"][:, col * LANES:(col + 1) * LANES])
        xdg_t = m["Xd"][:, g * gw:(g + 1) * gw].T.astype(BF16)
        contrib = _dot(xdg_t, m["bgs"][g])
        for hh in range(hpg):
            h = g * hpg + hh
            rows = pl.ds(h * SSD_HD, SSD_HD)
            state_ref[0, rows, :] = (state_ref[0, rows, :] * cd[h:h + 1, :]
                                     + contrib[hh * SSD_HD:(hh + 1) * SSD_HD, :])
    _ssd_finish(y_cols, m["xs"], z_ref[...], dskip_ref[...], normw_ref[...], out_ref)


def _ssd_consts(p):
    return [p["e64"], p["e128"], p["conv_w"], p["conv_b"], p["dt_bias"], p["a_log"], p["d_skip"], p["norm_w"]]


def _const_specs(nd_grid):
    z = (0, 0)
    if nd_grid == 2:
        f = lambda shape: pl.BlockSpec(shape, lambda b, c: z)
    else:
        f = lambda shape: pl.BlockSpec(shape, lambda i: z)
    return [f((3 * LANES, SSD_INNER)), f((3 * LANES, SSD_HEADS * LANES)), f((CONV_W, CONV_DIM)), f((1, CONV_DIM)),
            f((1, LANES)), f((1, LANES)), f((1, SSD_INNER)), f((1, SSD_INNER))]


def _ssd_prompt(xbc, dt, z, p, nbatch):
    T = xbc.shape[0]
    nc = T // nbatch // CHUNK
    row = lambda width: pl.BlockSpec((CHUNK, width), lambda b, c: (b * nc + c, 0))
    sq = pl.BlockSpec((CHUNK, 3 * CHUNK), lambda b, c: (0, 0))
    tri = jnp.tile(jnp.tril(jnp.ones((CHUNK, CHUNK), F32)).astype(BF16), (1, 3))
    ones = jnp.ones((CHUNK, 3 * CHUNK), BF16)
    return pl.pallas_call(
        _ssd_prompt_body,
        grid=(nbatch, nc),
        in_specs=[row(CONV_DIM), row(LANES), row(SSD_INNER), sq, sq] + _const_specs(2),
        out_specs=[row(SSD_INNER), pl.BlockSpec((1, SSD_HEADS * SSD_HD, D_STATE), lambda b, c: (b, 0, 0))],
        out_shape=[jax.ShapeDtypeStruct((T, SSD_INNER), BF16),
                   jax.ShapeDtypeStruct((nbatch, SSD_HEADS * SSD_HD, D_STATE), F32)],
        scratch_shapes=[pltpu.VMEM((8 + CHUNK, CONV_DIM), F32)],
        compiler_params=_cparams(("parallel", "arbitrary")), name="ssd_prompt",
    )(xbc, dt, z, tri, ones, *_ssd_consts(p))


SROWS = 8
SB = CHUNK // SROWS


def _ssd_sample_body(xbc_ref, dt_ref, z_ref, s0_ref, lmat_ref, mtot_ref, e64_ref, e128_ref, convw_ref, convb_ref,
                     dtb_ref, alog_ref, dskip_ref, normw_ref, out_ref, s1_ref, xpad_ref):
    xpad_ref[pl.ds(0, 8), :] = jnp.zeros((8, CONV_DIM), F32)
    xpad_ref[pl.ds(8, CHUNK), :] = xbc_ref[...]
    m = _ssd_chunk_math(xpad_ref, dt_ref[...], lmat_ref[...], mtot_ref[...], e64_ref[...], e128_ref[...],
                        convw_ref[...], convb_ref[...], dtb_ref[...], alog_ref[...])
    csT = m["csT"]
    hpg = SSD_HEADS // SSD_GROUPS
    gw = hpg * SSD_HD
    rowi = lax.broadcasted_iota(jnp.int32, (CHUNK, LANES), 0)
    last_tok = CONV_W - 1 + DEC_Q - 1
    xdT = [m["Xd"][:, g * gw:(g + 1) * gw].T.astype(BF16) for g in range(SSD_GROUPS)]
    yoff = [[jnp.zeros((CHUNK, LANES), F32) for _ in range(gw // LANES)] for _ in range(SSD_GROUPS)]
    for b in range(SB):
        lo = b * SROWS + (CONV_W - 1)
        tok = (rowi >= lo) & (rowi <= b * SROWS + last_tok)
        col = b * SROWS + last_tok
        cd = jnp.exp(jnp.broadcast_to(csT[0:SSD_HEADS, col:col + 1], (SSD_HEADS, LANES)))
        for g in range(SSD_GROUPS):
            sg = s0_ref[b, pl.ds(g * gw, gw), :]
            res = _dot_nt(m["cgs"][g], sg.astype(BF16))
            for jj in range(gw // LANES):
                yoff[g][jj] = jnp.where(tok, res[:, jj * LANES:(jj + 1) * LANES], yoff[g][jj])
            bmask = jnp.where(tok, m["bgs"][g], jnp.zeros_like(m["bgs"][g]))
            contrib = _dot(xdT[g], bmask)
            for hh in range(hpg):
                h = g * hpg + hh
                rows = pl.ds(h * SSD_HD, SSD_HD)
                s1_ref[b, rows, :] = (s0_ref[b, rows, :] * cd[h:h + 1, :]
                                      + contrib[hh * SSD_HD:(hh + 1) * SSD_HD, :])
    y_cols = []
    for g in range(SSD_GROUPS):
        for jj in range(gw // LANES):
            col = g * (gw // LANES) + jj
            y_cols.append(m["ydiag"][col] + yoff[g][jj] * m["ecs_e"][:, col * LANES:(col + 1) * LANES])
    _ssd_finish(y_cols, m["xs"], z_ref[...], dskip_ref[...], normw_ref[...], out_ref)


def _ssd_sample(xbc8, dt8, z8, s0, p):
    T8 = xbc8.shape[0]
    nblk = T8 // CHUNK
    row = lambda width: pl.BlockSpec((CHUNK, width), lambda i: (i, 0))
    sq = pl.BlockSpec((CHUNK, 3 * CHUNK), lambda i: (0, 0))
    st = pl.BlockSpec((SB, SSD_HEADS * SSD_HD, D_STATE), lambda i: (i, 0, 0))
    r = jnp.arange(CHUNK)
    same = (r[:, None] // SROWS) == (r[None, :] // SROWS)
    is_tok = ((r % SROWS) >= CONV_W - 1) & ((r % SROWS) < SROWS - 1)
    lmat = jnp.tile((same & (r[None, :] <= r[:, None]) & is_tok[None, :] & is_tok[:, None]).astype(BF16), (1, 3))
    mtot = jnp.tile((same & is_tok[None, :]).astype(BF16), (1, 3))
    return pl.pallas_call(
        _ssd_sample_body,
        grid=(nblk,),
        in_specs=[row(CONV_DIM), row(LANES), row(SSD_INNER), st, sq, sq] + _const_specs(1),
        out_specs=[row(SSD_INNER), st],
        out_shape=[jax.ShapeDtypeStruct((T8, SSD_INNER), BF16), jax.ShapeDtypeStruct(s0.shape, F32)],
        scratch_shapes=[pltpu.VMEM((8 + CHUNK, CONV_DIM), F32)],
        compiler_params=_cparams(("parallel",)), name="ssd_sample",
    )(xbc8, dt8, z8, s0, lmat, mtot, *_ssd_consts(p))


def _lam_value(lq1, lk1, lq2, lk2, lam_init):
    s1 = jnp.sum(lq1[...] * lk1[...], axis=1, keepdims=True)
    s2 = jnp.sum(lq2[...] * lk2[...], axis=1, keepdims=True)
    return jnp.exp(s1) - jnp.exp(s2) + lam_init


def _flash_body(q_ref, k_ref, vt_ref, lq1, lk1, lq2, lk2, sw_ref, o_ref, sa_ref, sb_ref, bma_ref, bmb_ref,
                m_ref, acc_ref, *, tq, tk, lam_init):
    i = pl.program_id(2)
    qt = q_ref[...].astype(F32).T
    row = lax.broadcasted_iota(jnp.int32, (LANES, tq), 0)
    qst = jnp.concatenate([jnp.where(row < ATT_HD, qt, 0.0), jnp.where(row >= ATT_HD, qt, 0.0)],
                          axis=1).astype(BF16)
    ones = jnp.ones((ONES_ROWS, tk), BF16)
    m_ref[...] = jnp.full(m_ref.shape, NEG, F32)
    acc_ref[...] = jnp.zeros(acc_ref.shape, F32)

    def produce(blk, s_ref, bm_ref):
        kblk = k_ref[pl.ds(pl.multiple_of(blk * tk, tk), tk), :]
        s = _dot(kblk, qst)
        s_ref[...] = s
        bm_ref[...] = jnp.max(s, axis=0, keepdims=True)

    def consume(blk, s_ref, bm_ref, off, cols=slice(None)):
        s = s_ref[:, cols]
        if off is None:
            bmax = bm_ref[:, cols]
        else:
            kpos = lax.broadcasted_iota(jnp.int32, s.shape, 0) + off
            qpos = lax.broadcasted_iota(jnp.int32, s.shape, 1) + (cols.start or 0)
            qpos = jnp.where(qpos >= tq, qpos - tq, qpos)
            s = jnp.where(kpos <= qpos, s, NEG)
            bmax = jnp.max(s, axis=0, keepdims=True)
        m_old = m_ref[:, cols]
        m_new = jnp.maximum(m_old, bmax)
        alpha = jnp.exp2(m_old - m_new)
        p = jnp.exp2((s - m_new).astype(BF16))
        m_ref[:, cols] = m_new
        vt1 = jnp.concatenate([vt_ref[0, blk], ones], axis=0)
        acc_ref[:, cols] = alpha * acc_ref[:, cols] + _dot(vt1, p)

    produce(0, sa_ref, bma_ref)

    def pair(p):
        produce(2 * p + 1, sb_ref, bmb_ref)
        consume(2 * p, sa_ref, bma_ref, None)
        produce(2 * p + 2, sa_ref, bma_ref)
        consume(2 * p + 1, sb_ref, bmb_ref, None)

    def body(pp, carry):
        pair(2 * pp)
        pair(2 * pp + 1)
        return carry

    lax.fori_loop(0, i // 2, body, 0)

    @pl.when(i % 2 == 1)
    def _():
        pair(i - 1)

    produce(2 * i + 1, sb_ref, bmb_ref)
    consume(2 * i, sa_ref, bma_ref, 0)
    consume(2 * i + 1, sb_ref, bmb_ref, tk, slice(tk, tq))
    consume(2 * i + 1, sb_ref, bmb_ref, tk, slice(tq + tk, 2 * tq))

    lam = _lam_value(lq1, lk1, lq2, lk2, lam_init)
    acc = acc_ref[0:LANES, :]
    l = acc_ref[LANES:LANES + 1, :]
    o = acc[:, :tq] / l[:, :tq] - lam * (acc[:, tq:] / l[:, tq:])
    inv = lax.rsqrt(jnp.mean(o * o, axis=0, keepdims=True) + EPS)
    o_ref[...] = ((o * inv).T * (sw_ref[...] * (1.0 - lam_init))).astype(o_ref.dtype)


def _flash(q, kb, vt, lams, subln, *, lam_init):
    T = q.shape[0]
    nb, nk, _, tk = vt.shape
    tq = 2 * tk
    L = T // nb
    nq = L // tq
    vec = pl.BlockSpec((1, ATT_HD), lambda b, h, i: (0, 0))
    return pl.pallas_call(
        functools.partial(_flash_body, tq=tq, tk=tk, lam_init=lam_init),
        grid=(nb, ATT_HEADS, nq),
        in_specs=[pl.BlockSpec((tq, LANES), lambda b, h, i: (b * nq + i, h)),
                  pl.BlockSpec((L, LANES), lambda b, h, i: (b, h)),
                  pl.BlockSpec((1, nk, LANES, tk), lambda b, h, i: (b, 0, h, 0)),
                  vec, vec, vec, vec, pl.BlockSpec((1, LANES), lambda b, h, i: (0, 0))],
        out_specs=pl.BlockSpec((tq, LANES), lambda b, h, i: (b * nq + i, h)),
        out_shape=jax.ShapeDtypeStruct((T, ATT_WIDTH), BF16),
        scratch_shapes=[pltpu.VMEM((tk, 2 * tq), F32), pltpu.VMEM((tk, 2 * tq), F32),
                        pltpu.VMEM((1, 2 * tq), F32), pltpu.VMEM((1, 2 * tq), F32),
                        pltpu.VMEM((1, 2 * tq), F32), pltpu.VMEM((LANES + ONES_ROWS, 2 * tq), F32)],
        compiler_params=_cparams(("parallel", "parallel", "arbitrary")), name="flash_diff",
    )(q, kb, vt, *lams, subln)


ONES_ROWS = 16
PAGES_PER_STEP = 16
DEC_Q = 4


def _decode_body(pt_ref, qbd_ref, kn_ref, vx_ref, *refs, pps, page, lam_init):
    k_refs = refs[:pps]
    v_refs = refs[pps:2 * pps]
    lq1, lk1, lq2, lk2, sw_ref, o_ref, m_ref, l_ref, acc_ref = refs[2 * pps:]
    j = pl.program_id(1)
    nsteps = pl.num_programs(1)
    rows_per_head = 2 * DEC_Q

    @pl.when(j == 0)
    def _():
        m_ref[...] = jnp.full(m_ref.shape, NEG, F32)
        l_ref[...] = jnp.zeros(l_ref.shape, F32)
        acc_ref[...] = jnp.zeros(acc_ref.shape, F32)

    qbd = qbd_ref[0]
    kk = jnp.concatenate([k_refs[i][0].reshape(2 * ATT_HEADS * ATT_HD, page).astype(BF16) for i in range(pps)],
                         axis=1)
    s = _dot(qbd, kk)
    m_old = m_ref[...]
    m_new = jnp.maximum(m_old, jnp.max(s, axis=1, keepdims=True))
    alpha = jnp.exp2(m_old - m_new)
    p = jnp.exp2(s - m_new)
    l_ref[...] = alpha * l_ref[...] + jnp.sum(p, axis=1, keepdims=True)
    m_ref[...] = m_new
    pb = p.astype(BF16)

    def head_values(h):
        return jnp.concatenate([v_refs[i][0, pl.ds(h, page, stride=ATT_HEADS), :] for i in range(pps)],
                               axis=0).astype(BF16)

    for h in range(0, ATT_HEADS, 2):
        vv = jnp.concatenate([head_values(h), head_values(h + 1)], axis=1)
        r0 = h * rows_per_head
        pv = _dot(pb[r0:r0 + 2 * rows_per_head, :], vv)
        for u in range(2):
            rows = pl.ds(r0 + u * rows_per_head, rows_per_head)
            acc_ref[rows, :] = (alpha[r0 + u * rows_per_head:r0 + (u + 1) * rows_per_head, :] * acc_ref[rows, :]
                                + pv[u * rows_per_head:(u + 1) * rows_per_head, u * LANES:(u + 1) * LANES])

    @pl.when(j == nsteps - 1)
    def _():
        hist = CONV_W - 1
        sn = _dot_nt(qbd, kn_ref[0].astype(BF16))
        r = lax.broadcasted_iota(jnp.int32, sn.shape, 0)
        t = lax.broadcasted_iota(jnp.int32, sn.shape, 1) - hist
        sn = jnp.where((t >= 0) & (t < DEC_Q) & (t <= r % DEC_Q), sn, NEG)
        m_o = m_ref[...]
        m_n = jnp.maximum(m_o, jnp.max(sn, axis=1, keepdims=True))
        a = jnp.exp2(m_o - m_n)
        pn = jnp.exp2(sn - m_n)
        l = a * l_ref[...] + jnp.sum(pn, axis=1, keepdims=True)
        acc = a * acc_ref[...]
        for tt in range(DEC_Q):
            acc = acc + pn[:, hist + tt:hist + tt + 1] * vx_ref[0, tt]
        lam = _lam_value(lq1, lk1, lq2, lk2, lam_init)
        o1 = acc / l
        nrow = o1.shape[0]
        o = o1 - lam * pltpu.roll(o1, nrow - DEC_Q, 0)
        inv = lax.rsqrt(jnp.mean(o * o, axis=1, keepdims=True) + EPS)
        o_ref[0] = (o * inv * (sw_ref[...] * (1.0 - lam_init))).astype(o_ref.dtype)


def _decode_attn(qbd, knew, vexp, ckt, cv, page_table, lams, subln, *, lam_init):
    nb, npages = page_table.shape
    page = ckt.shape[3]
    pps = PAGES_PER_STEP
    assert npages % pps == 0, "pages per sequence must be a multiple of PAGES_PER_STEP"
    nsteps = npages // pps
    nrow = 2 * ATT_HEADS * DEC_Q

    def kspec(i):
        return pl.BlockSpec((1, 2 * ATT_HEADS, ATT_HD, page), lambda b, j, pt: (pt[b, j * pps + i], 0, 0, 0))

    def vspec(i):
        return pl.BlockSpec((1, page * ATT_HEADS, LANES), lambda b, j, pt: (pt[b, j * pps + i], 0, 0))

    vec = pl.BlockSpec((1, ATT_HD), lambda b, j, pt: (0, 0))
    in_specs = ([pl.BlockSpec((1, nrow, 2 * ATT_HEADS * ATT_HD), lambda b, j, pt: (b, 0, 0)),
                 pl.BlockSpec((1, SROWS, 2 * ATT_HEADS * ATT_HD), lambda b, j, pt: (b, 0, 0)),
                 pl.BlockSpec((1, DEC_Q, nrow, LANES), lambda b, j, pt: (b, 0, 0, 0))]
                + [kspec(i) for i in range(pps)] + [vspec(i) for i in range(pps)]
                + [vec, vec, vec, vec, pl.BlockSpec((1, LANES), lambda b, j, pt: (0, 0))])
    grid_spec = pltpu.PrefetchScalarGridSpec(
        num_scalar_prefetch=1, grid=(nb, nsteps), in_specs=in_specs,
        out_specs=pl.BlockSpec((1, nrow, LANES), lambda b, j, pt: (b, 0, 0)),
        scratch_shapes=[pltpu.VMEM((nrow, 1), F32), pltpu.VMEM((nrow, 1), F32), pltpu.VMEM((nrow, LANES), F32)])
    return pl.pallas_call(
        functools.partial(_decode_body, pps=pps, page=page, lam_init=lam_init),
        grid_spec=grid_spec,
        out_shape=jax.ShapeDtypeStruct((nb, nrow, LANES), BF16),
        compiler_params=_cparams(("parallel", "arbitrary")), name="decode_attn",
    )(page_table, qbd, knew, vexp, *([ckt] * pps), *([cv] * pps), *lams, subln)


def _layer_norm(x, g, b):
    mu = jnp.mean(x, axis=1, keepdims=True)
    xc = x - mu
    var = jnp.mean(xc * xc, axis=1, keepdims=True)
    return xc * lax.rsqrt(var + EPS) * g + b


def _post_body(att_ref, ssd_ref, x_ref, pe_ref, woa_ref, wos_ref, wg_ref, wu_ref, wfo_ref, wpe_ref, wpg_ref,
               g1_ref, b1_ref, g2_ref, b2_ref, o_ref, *, alpha, nff):
    mix = _dot(att_ref[...], woa_ref[...]) + _dot(ssd_ref[...], wos_ref[...])
    h = _layer_norm(alpha * x_ref[...] + mix, g1_ref[...], b1_ref[...])
    hb = h.astype(BF16)
    ffn = None
    step = D_FF // nff
    for c in range(nff):
        sl = slice(c * step, (c + 1) * step)
        gate = _dot(hb, wg_ref[:, sl])
        up = _dot(hb, wu_ref[:, sl])
        part = _dot((_silu(gate) * up).astype(BF16), wfo_ref[sl, :])
        ffn = part if ffn is None else ffn + part
    pemb = _dot(pe_ref[...].astype(BF16), wpe_ref[...]) * jax.nn.sigmoid(_dot(hb, wpg_ref[...]))
    o_ref[...] = _layer_norm(alpha * h + ffn + pemb, g2_ref[...], b2_ref[...])


def _post(att, ssd, x2d, pe2d, w, *, tm, alpha):
    T = x2d.shape[0]
    row = lambda width: pl.BlockSpec((tm, width), lambda i: (i, 0))
    full = lambda shape: pl.BlockSpec(shape, lambda i: (0, 0), pipeline_mode=pl.Buffered(1))
    return pl.pallas_call(
        functools.partial(_post_body, alpha=alpha, nff=2),
        grid=(T // tm,),
        in_specs=[row(1024), row(1024), row(D_MODEL), row(P_DIM),
                  full((1024, D_MODEL)), full((1024, D_MODEL)), full((D_MODEL, D_FF)), full((D_MODEL, D_FF)),
                  full((D_FF, D_MODEL)), full((P_DIM, D_MODEL)), full((D_MODEL, D_MODEL)),
                  full((1, D_MODEL)), full((1, D_MODEL)), full((1, D_MODEL)), full((1, D_MODEL))],
        out_specs=row(D_MODEL),
        out_shape=jax.ShapeDtypeStruct((T, D_MODEL), F32),
        compiler_params=_cparams(("parallel",)), name="post",
    )(att, ssd, x2d, pe2d, w["woa"], w["wos"], w["wg"], w["wu"], w["wfo"], w["wpe"], w["wpg"],
      w["g1"], w["b1"], w["g2"], w["b2"])


def _rope_tables(pos):
    half = ROT_DIM // 2
    inv = ROPE_THETA ** (-jnp.arange(half, dtype=F32) * 2.0 / ROT_DIM)
    ang = pos.astype(F32)[:, None] * inv[None, :]
    cos, sin = jnp.cos(ang), jnp.sin(ang)
    n = pos.shape[0]
    pad = jnp.zeros((n, ATT_HD - ROT_DIM), F32)
    ra = jnp.concatenate([cos, cos, jnp.ones((n, ATT_HD - ROT_DIM), F32)], 1)
    rb = jnp.concatenate([-sin, jnp.zeros((n, half), F32), pad], 1)
    rc = jnp.concatenate([jnp.zeros((n, half), F32), sin, pad], 1)
    rep = LANES // ATT_HD
    return tuple(jnp.tile(t, (1, rep)) for t in (ra, rb, rc))


def _layer(depth, lam_init, x_prompt, x_sample, p_prompt, p_sample, cache_k, cache_v, page_table, state_ssm,
           state_conv, prm):
    (w_in, conv_w, conv_b, dt_bias, a_log, d_skip, ssd_norm_w, lq1, lk1, lq2, lk2, subln_w,
     w_out, ln1_g, ln1_b, w_ffn_in, w_ffn_out, w_pe, w_pg, ln2_g, ln2_b) = prm
    bp, lp, _ = x_prompt.shape
    bs, ls, _ = x_sample.shape
    npages = page_table.shape[1]
    page = cache_k.shape[1]
    past_len = npages * page
    alpha = (2 * depth) ** 0.25

    wb = w_in.astype(BF16)
    wq = wb[:, 0:1024]
    wk = wb[:, 1024:2048]
    wv = wb[:, 2048:3072]
    wz = wb[:, 3072:4096]
    wxbc = wb[:, 4096:4096 + CONV_DIM]
    wdt = jnp.pad(wb[:, 4096 + CONV_DIM:], ((0, 0), (0, LANES - SSD_HEADS)))
    w1 = dict(wq=wq, wk=wk, wv=wv, wz=wz, wxbc=wxbc, wdt=wdt)

    hrep = jnp.arange(SSD_HEADS * SSD_HD) // SSD_HD
    e64 = (jnp.arange(LANES)[:, None] == hrep[None, :]).astype(BF16)
    e128 = (jnp.arange(LANES)[:, None] == (jnp.arange(SSD_HEADS * LANES) // LANES)[None, :]).astype(BF16)
    padl = lambda v: jnp.pad(v.astype(F32), (0, LANES - v.shape[0]))[None, :]
    pssd = dict(e64=jnp.tile(e64, (3, 1)), e128=jnp.tile(e128, (3, 1)), conv_w=conv_w.astype(F32), conv_b=conv_b.astype(F32)[None, :],
                dt_bias=padl(dt_bias), a_log=padl(a_log), d_skip=jnp.repeat(d_skip.astype(F32), SSD_HD)[None, :],
                norm_w=ssd_norm_w.astype(F32)[None, :])
    lams = [v.astype(F32)[None, :] for v in (lq1, lk1, lq2, lk2)]
    subln = subln_w.astype(F32)[None, :]
    wob = w_out.astype(BF16)
    wfi = w_ffn_in.astype(BF16)
    w4 = dict(woa=wob[:ATT_WIDTH], wos=wob[ATT_WIDTH:], wg=wfi[:, :D_FF], wu=wfi[:, D_FF:],
              wfo=w_ffn_out.astype(BF16), wpe=w_pe.astype(BF16), wpg=w_pg.astype(BF16),
              g1=ln1_g.astype(F32)[None, :], b1=ln1_b.astype(F32)[None, :],
              g2=ln2_g.astype(F32)[None, :], b2=ln2_b.astype(F32)[None, :])

    tm = 256
    xp2 = x_prompt.reshape(bp * lp, D_MODEL)
    q, kt, vf, z, xbc, dt, kb, vt = _inproj(xp2, w1, _rope_tables(jnp.arange(lp)), tm=tm, nbatch=bp,
                                            attn_copies=True)
    k_p = jnp.transpose(kt.reshape(bp, 2 * ATT_HEADS, ATT_HD, lp), (0, 3, 1, 2))
    ssd_p, ssm_p = _ssd_prompt(xbc, dt, z, pssd, bp)
    att_p = _flash(q, kb, vt, lams, subln, lam_init=lam_init)
    y_p = _post(att_p, ssd_p, xp2, p_prompt.reshape(bp * lp, P_DIM), w4, tm=2 * tm, alpha=alpha)
    conv_p = xbc.reshape(bp, lp, CONV_DIM)[:, lp - (CONV_W - 1):, :]

    hist = CONV_W - 1
    x8 = jnp.pad(x_sample.astype(F32), ((0, 0), (hist, SROWS - hist - ls), (0, 0)))
    pos8 = jnp.tile(jnp.clip(jnp.arange(SROWS) - hist, 0, ls - 1) + past_len, bs)
    qs, kfs, vfs, zs, xbcs, dts = _inproj(x8.reshape(bs * SROWS, D_MODEL), w1, _rope_tables(pos8),
                                          tm=256, nbatch=1, attn_copies=False)
    xbc8 = jnp.concatenate([state_conv.astype(F32), xbcs.reshape(bs, SROWS, CONV_DIM)[:, hist:]], axis=1)
    ssd_s8, ssm_s = _ssd_sample(xbc8.reshape(bs * SROWS, CONV_DIM), dts, zs,
                                state_ssm.reshape(bs, SSD_HEADS * SSD_HD, D_STATE), pssd)
    tok = lambda a, w: a.reshape(bs, SROWS, w)[:, hist:hist + ls]
    ncomp = 2 * ATT_HEADS
    q4 = tok(qs, 1024).reshape(bs, ls, ncomp, ATT_HD)
    qcq = jnp.transpose(q4, (0, 2, 1, 3)).reshape(bs, ncomp * ls, ATT_HD)
    slot = (jnp.arange(ncomp * ls)[:, None] // ls) == (jnp.arange(ncomp * ATT_HD)[None, :] // ATT_HD)
    qbd = jnp.where(slot[None], jnp.tile(qcq, (1, 1, ncomp)), jnp.zeros((), BF16))
    k_s = tok(kfs, 1024)
    v_s = tok(vfs, 1024)
    vexp = jnp.repeat(v_s.reshape(bs, ls, ATT_HEADS, 2 * ATT_HD), 2 * ls, axis=2)
    att_s = _decode_attn(qbd, kfs.reshape(bs, SROWS, 1024), vexp,
                         jnp.transpose(cache_k, (0, 2, 3, 1)),
                         cache_v.reshape(cache_v.shape[0], page * ATT_HEADS, 2 * ATT_HD),
                         page_table, lams, subln, lam_init=lam_init)
    att_s = att_s.reshape(bs, ATT_HEADS, 2, ls, LANES)[:, :, 0]
    att_s = jnp.transpose(att_s, (0, 2, 1, 3)).reshape(bs * ls, ATT_WIDTH)
    ssd_s = tok(ssd_s8, SSD_INNER).reshape(bs * ls, SSD_INNER)
    y_s = _post(att_s, ssd_s, x_sample.reshape(bs * ls, D_MODEL), p_sample.reshape(bs * ls, P_DIM), w4,
                tm=min(256, bs * ls), alpha=alpha)
    conv_s = tok(xbcs, CONV_DIM)[:, ls - hist:]

    return (y_p.reshape(bp, lp, D_MODEL), y_s.reshape(bs, ls, D_MODEL),
            k_p, vf.reshape(bp, lp, ATT_HEADS, 2 * ATT_HD),
            ssm_p.reshape(bp, SSD_HEADS, SSD_HD, D_STATE), conv_p,
            k_s.reshape(bs, ls, 2 * ATT_HEADS, ATT_HD), v_s.reshape(bs, ls, ATT_HEADS, 2 * ATT_HD),
            ssm_s.reshape(bs, SSD_HEADS, SSD_HD, D_STATE), conv_s)


def kernel(x_prompt, x_sample, p_prompt, p_sample, cache_k, cache_v, page_table, state_ssm, state_conv, w_in, conv_w, conv_b, dt_bias, a_log, d_skip, ssd_norm_w, lambda_q1, lambda_k1, lambda_q2, lambda_k2, subln_w, w_out, ln1_g, ln1_b, w_ffn_in, w_ffn_out, w_pe, w_pg, ln2_g, ln2_b):
    depth = w_in.shape[0]
    assert depth == 1, "single-layer trunk"
    prm = tuple(a[0] for a in (w_in, conv_w, conv_b, dt_bias, a_log, d_skip, ssd_norm_w, lambda_q1, lambda_k1,
                               lambda_q2, lambda_k2, subln_w, w_out, ln1_g, ln1_b, w_ffn_in, w_ffn_out, w_pe,
                               w_pg, ln2_g, ln2_b))
    lam_init = 0.8 - 0.6 * math.exp(-0.3 * 0)
    outs = _layer(depth, lam_init, x_prompt, x_sample, p_prompt[0], p_sample[0], cache_k[0], cache_v[0], page_table,
                  state_ssm[0], state_conv[0], prm)
    y_p, y_s = outs[0], outs[1]
    return (y_p, y_s) + tuple(o[None] for o in outs[2:])
```

```python
import functools
import math

import jax
import jax.numpy as jnp
from jax import lax
from jax.experimental import pallas as pl
from jax.experimental.pallas import tpu as pltpu

F32 = jnp.float32
BF16 = jnp.bfloat16

D_MODEL = 1024
ATT_HD = 64
ATT_HEADS = 8
ATT_WIDTH = 1024
ROT_DIM = 16
ROPE_THETA = 500000.0
SSD_HD = 64
SSD_INNER = 1024
SSD_HEADS = 16
SSD_GROUPS = 2
D_STATE = 128
CONV_W = 4
CONV_DIM = 1536
CHUNK = 128
D_FF = 2816
P_DIM = 256
EPS = 1e-5
LANES = 128
NEG = -1e30
Q_SCALE = (ATT_HD ** -0.5) * math.log2(math.e)

VMEM_LIMIT = 56 * 1024 * 1024


def _cparams(sem):
    return pltpu.CompilerParams(dimension_semantics=sem, vmem_limit_bytes=VMEM_LIMIT)


def _dot(a, b):
    return jnp.dot(a, b, preferred_element_type=F32)


def _dot_nt(a, b):
    return lax.dot_general(a, b, (((1,), (1,)), ((), ())), preferred_element_type=F32)


def _split3(a):
    hi = a.astype(BF16)
    r = a - hi.astype(F32)
    mid = r.astype(BF16)
    lo = (r - mid.astype(F32)).astype(BF16)
    return hi, mid, lo


def _dot_sel_lhs(sel3, a):
    hi, mid, lo = _split3(a)
    return _dot(sel3, jnp.concatenate([hi, mid, lo], axis=0))


def _dot_sel_rhs(a, sel3):
    hi, mid, lo = _split3(a)
    return _dot(jnp.concatenate([hi, mid, lo], axis=1), sel3)


def _silu(x):
    h = 0.5 * x
    return h + h * jnp.tanh(h)


def _softplus(x):
    return jnp.maximum(x, 0.0) + jnp.log1p(jnp.exp(-jnp.abs(x)))


def _rope_cols(t, ra, rb, rc):
    outs = []
    for j in range(t.shape[1] // LANES):
        c = t[:, j * LANES:(j + 1) * LANES]
        outs.append(c * ra + pltpu.roll(c, LANES - ROT_DIM // 2, 1) * rb + pltpu.roll(c, ROT_DIM // 2, 1) * rc)
    return outs


def _inproj_body(x_ref, wq_ref, wk_ref, wv_ref, wz_ref, wxbc_ref, wdt_ref, ra_ref, rb_ref, rc_ref,
                 q_ref, kf_ref, vf_ref, z_ref, xbc_ref, dt_ref, *maybe_bf16, attn_copies):
    xb = x_ref[...].astype(BF16)
    ra = ra_ref[...]
    rb = rb_ref[...]
    rc = rc_ref[...]
    q = _dot(xb, wq_ref[...])
    for j, c in enumerate(_rope_cols(q, ra, rb, rc)):
        q_ref[:, j * LANES:(j + 1) * LANES] = (c * Q_SCALE).astype(BF16)
    k = _dot(xb, wk_ref[...])
    for j, c in enumerate(_rope_cols(k, ra, rb, rc)):
        if attn_copies:
            maybe_bf16[0][:, j * LANES:(j + 1) * LANES] = c.astype(BF16)
            kf_ref[0, j * LANES:(j + 1) * LANES, :] = c.T
        else:
            kf_ref[:, j * LANES:(j + 1) * LANES] = c
    v = _dot(xb, wv_ref[...])
    vf_ref[...] = v
    if attn_copies:
        for j in range(v.shape[1] // LANES):
            maybe_bf16[1][0, 0, j * LANES:(j + 1) * LANES, :] = v[:, j * LANES:(j + 1) * LANES].T.astype(BF16)
    z_ref[...] = _dot(xb, wz_ref[...])
    xbc_ref[...] = _dot(xb, wxbc_ref[...])
    dt_ref[...] = _dot(xb, wdt_ref[...])


def _inproj(x2d, w, tabs, *, tm, nbatch, attn_copies):
    T = x2d.shape[0]
    nt = T // tm
    ntab = tabs[0].shape[0] // tm
    per_b = nt // nbatch
    full = lambda shape: pl.BlockSpec(shape, lambda i: (0,) * len(shape))
    row = lambda width: pl.BlockSpec((tm, width), lambda i: (i, 0))
    tab = pl.BlockSpec((tm, LANES), lambda i: (i % ntab, 0))
    in_specs = [row(D_MODEL), full((D_MODEL, 1024)), full((D_MODEL, 1024)), full((D_MODEL, 1024)),
                full((D_MODEL, 1024)), full((D_MODEL, CONV_DIM)), full((D_MODEL, LANES)), tab, tab, tab]
    k_shape, k_spec = jax.ShapeDtypeStruct((T, 1024), F32), row(1024)
    if attn_copies:
        k_shape = jax.ShapeDtypeStruct((nbatch, 1024, T // nbatch), F32)
        k_spec = pl.BlockSpec((1, 1024, tm), lambda i: (i // per_b, 0, i % per_b))
    out_shape = [jax.ShapeDtypeStruct((T, 1024), BF16), k_shape,
                 jax.ShapeDtypeStruct((T, 1024), F32), jax.ShapeDtypeStruct((T, 1024), F32),
                 jax.ShapeDtypeStruct((T, CONV_DIM), F32), jax.ShapeDtypeStruct((T, LANES), F32)]
    out_specs = [row(1024), k_spec, row(1024), row(1024), row(CONV_DIM), row(LANES)]
    if attn_copies:
        out_shape += [jax.ShapeDtypeStruct((T, 1024), BF16),
                      jax.ShapeDtypeStruct((nbatch, per_b, 1024, tm), BF16)]
        out_specs += [row(1024), pl.BlockSpec((1, 1, 1024, tm), lambda i: (i // per_b, i % per_b, 0, 0))]
    return pl.pallas_call(
        functools.partial(_inproj_body, attn_copies=attn_copies),
        grid=(nt,), in_specs=in_specs, out_specs=out_specs, out_shape=out_shape,
        compiler_params=_cparams(("parallel",)), name="inproj",
    )(x2d, w["wq"], w["wk"], w["wv"], w["wz"], w["wxbc"], w["wdt"], *tabs)


def _ssd_chunk_math(xpad_ref, dt_raw, lmat, mtot, e64, e128, convw, convb, dtb, alog):
    acc = convb
    for j in range(CONV_W):
        acc = acc + xpad_ref[pl.ds(8 - (CONV_W - 1) + j, CHUNK), :] * convw[j:j + 1, :]
    act = _silu(acc)
    xs = act[:, :SSD_INNER]
    lane = lax.broadcasted_iota(jnp.int32, (1, LANES), 1)
    a_row = jnp.where(lane < SSD_HEADS, -jnp.exp(alog), 0.0)
    dtv = _softplus(dt_raw + dtb)
    dA = dtv * a_row
    cs = _dot_sel_lhs(lmat, dA)
    cst = _dot_sel_lhs(mtot, dA)
    csT = cs.T
    ex = _dot_sel_rhs(jnp.concatenate([dtv, cs, cst], axis=0), e64)
    dt_e = ex[0:CHUNK]
    cs_e = ex[CHUNK:2 * CHUNK]
    cst_e = ex[2 * CHUNK:3 * CHUNK]
    cs_b = _dot_sel_rhs(cs, e128)
    X = xs * dt_e
    Xd = X * jnp.exp(cst_e - cs_e)
    ecs_e = jnp.exp(cs_e)
    lbool = lmat[:, 0:CHUNK].astype(F32) > 0.5
    lane2 = lax.broadcasted_iota(jnp.int32, (CHUNK, LANES), 1)
    ydiag = []
    bgs, cgs = [], []
    for g in range(SSD_GROUPS):
        bg = act[:, SSD_INNER + g * D_STATE:SSD_INNER + (g + 1) * D_STATE].astype(BF16)
        cg = act[:, SSD_INNER + (SSD_GROUPS + g) * D_STATE:SSD_INNER + (SSD_GROUPS + g + 1) * D_STATE].astype(BF16)
        bgs.append(bg)
        cgs.append(cg)
        cb = _dot_nt(cg, bg)
        for jj in range(SSD_HEADS // SSD_GROUPS // 2):
            h0 = g * (SSD_HEADS // SSD_GROUPS) + 2 * jj
            xp = X[:, h0 * SSD_HD:h0 * SSD_HD + LANES].astype(BF16)
            yp = []
            for h in (h0, h0 + 1):
                seg = cs_b[:, h * LANES:(h + 1) * LANES] - csT[h:h + 1, :]
                lm = jnp.exp(jnp.where(lbool, seg, NEG))
                yp.append(_dot((cb * lm).astype(BF16), xp))
            ydiag.append(jnp.where(lane2 < SSD_HD, yp[0], yp[1]))
    return dict(act=act, xs=xs, Xd=Xd, ecs_e=ecs_e, csT=csT, ydiag=ydiag, bgs=bgs, cgs=cgs)


def _ssd_finish(y_cols, xs, z, dskip, normw, out_ref):
    gz = []
    for j in range(SSD_INNER // LANES):
        sl = slice(j * LANES, (j + 1) * LANES)
        y = y_cols[j] + dskip[:, sl] * xs[:, sl]
        gz.append(y * _silu(z[:, sl]))
    per_g = SSD_INNER // SSD_GROUPS // LANES
    for g in range(SSD_GROUPS):
        blk = gz[g * per_g:(g + 1) * per_g]
        ss = sum(jnp.sum(b * b, axis=1, keepdims=True) for b in blk)
        inv = lax.rsqrt(ss / (SSD_INNER // SSD_GROUPS) + EPS)
        for j, b in enumerate(blk):
            sl = slice((g * per_g + j) * LANES, (g * per_g + j + 1) * LANES)
            out_ref[:, sl] = (b * inv * normw[:, sl]).astype(out_ref.dtype)


def _ssd_prompt_body(xbc_ref, dt_ref, z_ref, lmat_ref, mtot_ref, e64_ref, e128_ref, convw_ref, convb_ref,
                     dtb_ref, alog_ref, dskip_ref, normw_ref, out_ref, state_ref, xpad_ref):
    c = pl.program_id(1)

    @pl.when(c == 0)
    def _():
        xpad_ref[pl.ds(0, 8), :] = jnp.zeros((8, CONV_DIM), F32)
        state_ref[...] = jnp.zeros(state_ref.shape, F32)

    xpad_ref[pl.ds(8, CHUNK), :] = xbc_ref[...]
    m = _ssd_chunk_math(xpad_ref, dt_ref[...], lmat_ref[...], mtot_ref[...], e64_ref[...], e128_ref[...],
                        convw_ref[...], convb_ref[...], dtb_ref[...], alog_ref[...])
    xpad_ref[pl.ds(8 - (CONV_W - 1), CONV_W - 1), :] = xpad_ref[pl.ds(8 + CHUNK - (CONV_W - 1), CONV_W - 1), :]

    csT = m["csT"]
    cd = jnp.exp(jnp.broadcast_to(csT[0:SSD_HEADS, CHUNK - 1:CHUNK], (SSD_HEADS, LANES)))
    hpg = SSD_HEADS // SSD_GROUPS
    gw = hpg * SSD_HD
    y_cols = []
    for g in range(SSD_GROUPS):
        sg = state_ref[0, pl.ds(g * gw, gw), :]
        yoff = _dot_nt(m["cgs"][g], sg.astype(BF16))
        for jj in range(gw // LANES):
            col = g * (gw // LANES) + jj
            y_cols.append(m["ydiag"][col] + yoff[:, jj * LANES:(jj + 1) * LANES]
                          * m["ecs_e"][:, col * LANES:(col + 1) * LANES])
        xdg_t = m["Xd"][:, g * gw:(g + 1) * gw].T.astype(BF16)
        contrib = _dot(xdg_t, m["bgs"][g])
        for hh in range(hpg):
            h = g * hpg + hh
            rows = pl.ds(h * SSD_HD, SSD_HD)
            state_ref[0, rows, :] = (state_ref[0, rows, :] * cd[h:h + 1, :]
                                     + contrib[hh * SSD_HD:(hh + 1) * SSD_HD, :])
    _ssd_finish(y_cols, m["xs"], z_ref[...], dskip_ref[...], normw_ref[...], out_ref)


def _ssd_consts(p):
    return [p["e64"], p["e128"], p["conv_w"], p["conv_b"], p["dt_bias"], p["a_log"], p["d_skip"], p["norm_w"]]


def _const_specs(nd_grid):
    z = (0, 0)
    if nd_grid == 2:
        f = lambda shape: pl.BlockSpec(shape, lambda b, c: z)
    else:
        f = lambda shape: pl.BlockSpec(shape, lambda i: z)
    return [f((3 * LANES, SSD_INNER)), f((3 * LANES, SSD_HEADS * LANES)), f((CONV_W, CONV_DIM)), f((1, CONV_DIM)),
            f((1, LANES)), f((1, LANES)), f((1, SSD_INNER)), f((1, SSD_INNER))]


def _ssd_prompt(xbc, dt, z, p, nbatch):
    T = xbc.shape[0]
    nc = T // nbatch // CHUNK
    row = lambda width: pl.BlockSpec((CHUNK, width), lambda b, c: (b * nc + c, 0))
    sq = pl.BlockSpec((CHUNK, 3 * CHUNK), lambda b, c: (0, 0))
    tri = jnp.tile(jnp.tril(jnp.ones((CHUNK, CHUNK), F32)).astype(BF16), (1, 3))
    ones = jnp.ones((CHUNK, 3 * CHUNK), BF16)
    return pl.pallas_call(
        _ssd_prompt_body,
        grid=(nbatch, nc),
        in_specs=[row(CONV_DIM), row(LANES), row(SSD_INNER), sq, sq] + _const_specs(2),
        out_specs=[row(SSD_INNER), pl.BlockSpec((1, SSD_HEADS * SSD_HD, D_STATE), lambda b, c: (b, 0, 0))],
        out_shape=[jax.ShapeDtypeStruct((T, SSD_INNER), BF16),
                   jax.ShapeDtypeStruct((nbatch, SSD_HEADS * SSD_HD, D_STATE), F32)],
        scratch_shapes=[pltpu.VMEM((8 + CHUNK, CONV_DIM), F32)],
        compiler_params=_cparams(("parallel", "arbitrary")), name="ssd_prompt",
    )(xbc, dt, z, tri, ones, *_ssd_consts(p))


SROWS = 8
SB = CHUNK // SROWS


def _ssd_sample_body(xbc_ref, dt_ref, z_ref, s0_ref, lmat_ref, mtot_ref, e64_ref, e128_ref, convw_ref, convb_ref,
                     dtb_ref, alog_ref, dskip_ref, normw_ref, out_ref, s1_ref, xpad_ref):
    xpad_ref[pl.ds(0, 8), :] = jnp.zeros((8, CONV_DIM), F32)
    xpad_ref[pl.ds(8, CHUNK), :] = xbc_ref[...]
    m = _ssd_chunk_math(xpad_ref, dt_ref[...], lmat_ref[...], mtot_ref[...], e64_ref[...], e128_ref[...],
                        convw_ref[...], convb_ref[...], dtb_ref[...], alog_ref[...])
    csT = m["csT"]
    hpg = SSD_HEADS // SSD_GROUPS
    gw = hpg * SSD_HD
    rowi = lax.broadcasted_iota(jnp.int32, (CHUNK, LANES), 0)
    last_tok = CONV_W - 1 + DEC_Q - 1
    xdT = [m["Xd"][:, g * gw:(g + 1) * gw].T.astype(BF16) for g in range(SSD_GROUPS)]
    yoff = [[jnp.zeros((CHUNK, LANES), F32) for _ in range(gw // LANES)] for _ in range(SSD_GROUPS)]
    for b in range(SB):
        lo = b * SROWS + (CONV_W - 1)
        tok = (rowi >= lo) & (rowi <= b * SROWS + last_tok)
        col = b * SROWS + last_tok
        cd = jnp.exp(jnp.broadcast_to(csT[0:SSD_HEADS, col:col + 1], (SSD_HEADS, LANES)))
        for g in range(SSD_GROUPS):
            sg = s0_ref[b, pl.ds(g * gw, gw), :]
            res = _dot_nt(m["cgs"][g], sg.astype(BF16))
            for jj in range(gw // LANES):
                yoff[g][jj] = jnp.where(tok, res[:, jj * LANES:(jj + 1) * LANES], yoff[g][jj])
            bmask = jnp.where(tok, m["bgs"][g], jnp.zeros_like(m["bgs"][g]))
            contrib = _dot(xdT[g], bmask)
            for hh in range(hpg):
                h = g * hpg + hh
                rows = pl.ds(h * SSD_HD, SSD_HD)
                s1_ref[b, rows, :] = (s0_ref[b, rows, :] * cd[h:h + 1, :]
                                      + contrib[hh * SSD_HD:(hh + 1) * SSD_HD, :])
    y_cols = []
    for g in range(SSD_GROUPS):
        for jj in range(gw // LANES):
            col = g * (gw // LANES) + jj
            y_cols.append(m["ydiag"][col] + yoff[g][jj] * m["ecs_e"][:, col * LANES:(col + 1) * LANES])
    _ssd_finish(y_cols, m["xs"], z_ref[...], dskip_ref[...], normw_ref[...], out_ref)


def _ssd_sample(xbc8, dt8, z8, s0, p):
    T8 = xbc8.shape[0]
    nblk = T8 // CHUNK
    row = lambda width: pl.BlockSpec((CHUNK, width), lambda i: (i, 0))
    sq = pl.BlockSpec((CHUNK, 3 * CHUNK), lambda i: (0, 0))
    st = pl.BlockSpec((SB, SSD_HEADS * SSD_HD, D_STATE), lambda i: (i, 0, 0))
    r = jnp.arange(CHUNK)
    same = (r[:, None] // SROWS) == (r[None, :] // SROWS)
    is_tok = ((r % SROWS) >= CONV_W - 1) & ((r % SROWS) < SROWS - 1)
    lmat = jnp.tile((same & (r[None, :] <= r[:, None]) & is_tok[None, :] & is_tok[:, None]).astype(BF16), (1, 3))
    mtot = jnp.tile((same & is_tok[None, :]).astype(BF16), (1, 3))
    return pl.pallas_call(
        _ssd_sample_body,
        grid=(nblk,),
        in_specs=[row(CONV_DIM), row(LANES), row(SSD_INNER), st, sq, sq] + _const_specs(1),
        out_specs=[row(SSD_INNER), st],
        out_shape=[jax.ShapeDtypeStruct((T8, SSD_INNER), BF16), jax.ShapeDtypeStruct(s0.shape, F32)],
        scratch_shapes=[pltpu.VMEM((8 + CHUNK, CONV_DIM), F32)],
        compiler_params=_cparams(("parallel",)), name="ssd_sample",
    )(xbc8, dt8, z8, s0, lmat, mtot, *_ssd_consts(p))


def _lam_value(lq1, lk1, lq2, lk2, lam_init):
    s1 = jnp.sum(lq1[...] * lk1[...], axis=1, keepdims=True)
    s2 = jnp.sum(lq2[...] * lk2[...], axis=1, keepdims=True)
    return jnp.exp(s1) - jnp.exp(s2) + lam_init


def _flash_body(q_ref, k_ref, vt_ref, lq1, lk1, lq2, lk2, sw_ref, o_ref, sa_ref, sb_ref, bma_ref, bmb_ref,
                m_ref, acc_ref, *, tq, tk, lam_init):
    i = pl.program_id(2)
    qt = q_ref[...].astype(F32).T
    row = lax.broadcasted_iota(jnp.int32, (LANES, tq), 0)
    qst = jnp.concatenate([jnp.where(row < ATT_HD, qt, 0.0), jnp.where(row >= ATT_HD, qt, 0.0)],
                          axis=1).astype(BF16)
    ones = jnp.ones((ONES_ROWS, tk), BF16)
    m_ref[...] = jnp.full(m_ref.shape, NEG, F32)
    acc_ref[...] = jnp.zeros(acc_ref.shape, F32)

    def produce(blk, s_ref, bm_ref, cols=slice(None)):
        kblk = k_ref[pl.ds(pl.multiple_of(blk * tk, tk), tk), :]
        s = _dot(kblk, qst[:, cols])
        s_ref[:, cols] = s
        bm_ref[:, cols] = jnp.max(s, axis=0, keepdims=True)

    def consume(blk, s_ref, bm_ref, off, cols=slice(None)):
        s = s_ref[:, cols]
        if off is None:
            bmax = bm_ref[:, cols]
        else:
            kpos = lax.broadcasted_iota(jnp.int32, s.shape, 0) + off
            qpos = lax.broadcasted_iota(jnp.int32, s.shape, 1) + (cols.start or 0)
            qpos = jnp.where(qpos >= tq, qpos - tq, qpos)
            s = jnp.where(kpos <= qpos, s, NEG)
            bmax = jnp.max(s, axis=0, keepdims=True)
        m_old = m_ref[:, cols]
        m_new = jnp.maximum(m_old, bmax)
        alpha = jnp.exp2(m_old - m_new)
        p = jnp.exp2((s - m_new).astype(BF16))
        m_ref[:, cols] = m_new
        vt1 = jnp.concatenate([vt_ref[0, blk], ones], axis=0)
        acc_ref[:, cols] = alpha * acc_ref[:, cols] + _dot(vt1, p)

    produce(0, sa_ref, bma_ref)

    def pair(p):
        parts = [slice(c * tq // 2, (c + 1) * tq // 2) for c in range(4)]
        for part in parts:
            produce(2 * p + 1, sb_ref, bmb_ref, part)
            consume(2 * p, sa_ref, bma_ref, None, part)
        for part in parts:
            produce(2 * p + 2, sa_ref, bma_ref, part)
            consume(2 * p + 1, sb_ref, bmb_ref, None, part)

    def body(pp, carry):
        pair(2 * pp)
        pair(2 * pp + 1)
        return carry

    lax.fori_loop(0, i // 2, body, 0)

    @pl.when(i % 2 == 1)
    def _():
        pair(i - 1)

    produce(2 * i + 1, sb_ref, bmb_ref)
    consume(2 * i, sa_ref, bma_ref, 0)
    consume(2 * i + 1, sb_ref, bmb_ref, tk, slice(tk, tq))
    consume(2 * i + 1, sb_ref, bmb_ref, tk, slice(tq + tk, 2 * tq))

    lam = _lam_value(lq1, lk1, lq2, lk2, lam_init)
    acc = acc_ref[0:LANES, :]
    l = acc_ref[LANES:LANES + 1, :]
    o = acc[:, :tq] / l[:, :tq] - lam * (acc[:, tq:] / l[:, tq:])
    inv = lax.rsqrt(jnp.mean(o * o, axis=0, keepdims=True) + EPS)
    o_ref[...] = ((o * inv).T * (sw_ref[...] * (1.0 - lam_init))).astype(o_ref.dtype)


def _flash(q, kb, vt, lams, subln, *, lam_init):
    T = q.shape[0]
    nb, nk, _, tk = vt.shape
    tq = 2 * tk
    L = T // nb
    nq = L // tq
    vec = pl.BlockSpec((1, ATT_HD), lambda b, h, i: (0, 0))
    return pl.pallas_call(
        functools.partial(_flash_body, tq=tq, tk=tk, lam_init=lam_init),
        grid=(nb, ATT_HEADS, nq),
        in_specs=[pl.BlockSpec((tq, LANES), lambda b, h, i: (b * nq + i, h)),
                  pl.BlockSpec((L, LANES), lambda b, h, i: (b, h)),
                  pl.BlockSpec((1, nk, LANES, tk), lambda b, h, i: (b, 0, h, 0)),
                  vec, vec, vec, vec, pl.BlockSpec((1, LANES), lambda b, h, i: (0, 0))],
        out_specs=pl.BlockSpec((tq, LANES), lambda b, h, i: (b * nq + i, h)),
        out_shape=jax.ShapeDtypeStruct((T, ATT_WIDTH), BF16),
        scratch_shapes=[pltpu.VMEM((tk, 2 * tq), F32), pltpu.VMEM((tk, 2 * tq), F32),
                        pltpu.VMEM((1, 2 * tq), F32), pltpu.VMEM((1, 2 * tq), F32),
                        pltpu.VMEM((1, 2 * tq), F32), pltpu.VMEM((LANES + ONES_ROWS, 2 * tq), F32)],
        compiler_params=_cparams(("parallel", "parallel", "arbitrary")), name="flash_diff",
    )(q, kb, vt, *lams, subln)


ONES_ROWS = 16
PAGES_PER_STEP = 16
DEC_Q = 4


def _decode_body(pt_ref, qbd_ref, kn_ref, vx_ref, *refs, pps, page, lam_init):
    k_refs = refs[:pps]
    v_refs = refs[pps:2 * pps]
    lq1, lk1, lq2, lk2, sw_ref, o_ref, m_ref, l_ref, acc_ref = refs[2 * pps:]
    j = pl.program_id(1)
    nsteps = pl.num_programs(1)
    rows_per_head = 2 * DEC_Q

    @pl.when(j == 0)
    def _():
        m_ref[...] = jnp.full(m_ref.shape, NEG, F32)
        l_ref[...] = jnp.zeros(l_ref.shape, F32)
        acc_ref[...] = jnp.zeros(acc_ref.shape, F32)

    qbd = qbd_ref[0]
    kk = jnp.concatenate([k_refs[i][0].reshape(2 * ATT_HEADS * ATT_HD, page).astype(BF16) for i in range(pps)],
                         axis=1)
    s = _dot(qbd, kk)
    m_old = m_ref[...]
    m_new = jnp.maximum(m_old, jnp.max(s, axis=1, keepdims=True))
    alpha = jnp.exp2(m_old - m_new)
    p = jnp.exp2(s - m_new)
    l_ref[...] = alpha * l_ref[...] + jnp.sum(p, axis=1, keepdims=True)
    m_ref[...] = m_new
    pb = p.astype(BF16)

    def head_values(h):
        return jnp.concatenate([v_refs[i][0, pl.ds(h, page, stride=ATT_HEADS), :] for i in range(pps)],
                               axis=0).astype(BF16)

    for h in range(0, ATT_HEADS, 2):
        vv = jnp.concatenate([head_values(h), head_values(h + 1)], axis=1)
        r0 = h * rows_per_head
        pv = _dot(pb[r0:r0 + 2 * rows_per_head, :], vv)
        for u in range(2):
            rows = pl.ds(r0 + u * rows_per_head, rows_per_head)
            acc_ref[rows, :] = (alpha[r0 + u * rows_per_head:r0 + (u + 1) * rows_per_head, :] * acc_ref[rows, :]
                                + pv[u * rows_per_head:(u + 1) * rows_per_head, u * LANES:(u + 1) * LANES])

    @pl.when(j == nsteps - 1)
    def _():
        hist = CONV_W - 1
        sn = _dot_nt(qbd, kn_ref[0].astype(BF16))
        r = lax.broadcasted_iota(jnp.int32, sn.shape, 0)
        t = lax.broadcasted_iota(jnp.int32, sn.shape, 1) - hist
        sn = jnp.where((t >= 0) & (t < DEC_Q) & (t <= r % DEC_Q), sn, NEG)
        m_o = m_ref[...]
        m_n = jnp.maximum(m_o, jnp.max(sn, axis=1, keepdims=True))
        a = jnp.exp2(m_o - m_n)
        pn = jnp.exp2(sn - m_n)
        l = a * l_ref[...] + jnp.sum(pn, axis=1, keepdims=True)
        acc = a * acc_ref[...]
        for tt in range(DEC_Q):
            acc = acc + pn[:, hist + tt:hist + tt + 1] * vx_ref[0, tt]
        lam = _lam_value(lq1, lk1, lq2, lk2, lam_init)
        o1 = acc / l
        nrow = o1.shape[0]
        o = o1 - lam * pltpu.roll(o1, nrow - DEC_Q, 0)
        inv = lax.rsqrt(jnp.mean(o * o, axis=1, keepdims=True) + EPS)
        o_ref[0] = (o * inv * (sw_ref[...] * (1.0 - lam_init))).astype(o_ref.dtype)


def _decode_attn(qbd, knew, vexp, ckt, cv, page_table, lams, subln, *, lam_init):
    nb, npages = page_table.shape
    page = ckt.shape[3]
    pps = PAGES_PER_STEP
    assert npages % pps == 0, "pages per sequence must be a multiple of PAGES_PER_STEP"
    nsteps = npages // pps
    nrow = 2 * ATT_HEADS * DEC_Q

    def kspec(i):
        return pl.BlockSpec((1, 2 * ATT_HEADS, ATT_HD, page), lambda b, j, pt: (pt[b, j * pps + i], 0, 0, 0))

    def vspec(i):
        return pl.BlockSpec((1, page * ATT_HEADS, LANES), lambda b, j, pt: (pt[b, j * pps + i], 0, 0))

    vec = pl.BlockSpec((1, ATT_HD), lambda b, j, pt: (0, 0))
    in_specs = ([pl.BlockSpec((1, nrow, 2 * ATT_HEADS * ATT_HD), lambda b, j, pt: (b, 0, 0)),
                 pl.BlockSpec((1, SROWS, 2 * ATT_HEADS * ATT_HD), lambda b, j, pt: (b, 0, 0)),
                 pl.BlockSpec((1, DEC_Q, nrow, LANES), lambda b, j, pt: (b, 0, 0, 0))]
                + [kspec(i) for i in range(pps)] + [vspec(i) for i in range(pps)]
                + [vec, vec, vec, vec, pl.BlockSpec((1, LANES), lambda b, j, pt: (0, 0))])
    grid_spec = pltpu.PrefetchScalarGridSpec(
        num_scalar_prefetch=1, grid=(nb, nsteps), in_specs=in_specs,
        out_specs=pl.BlockSpec((1, nrow, LANES), lambda b, j, pt: (b, 0, 0)),
        scratch_shapes=[pltpu.VMEM((nrow, 1), F32), pltpu.VMEM((nrow, 1), F32), pltpu.VMEM((nrow, LANES), F32)])
    return pl.pallas_call(
        functools.partial(_decode_body, pps=pps, page=page, lam_init=lam_init),
        grid_spec=grid_spec,
        out_shape=jax.ShapeDtypeStruct((nb, nrow, LANES), BF16),
        compiler_params=_cparams(("parallel", "arbitrary")), name="decode_attn",
    )(page_table, qbd, knew, vexp, *([ckt] * pps), *([cv] * pps), *lams, subln)


def _layer_norm(x, g, b):
    mu = jnp.mean(x, axis=1, keepdims=True)
    xc = x - mu
    var = jnp.mean(xc * xc, axis=1, keepdims=True)
    return xc * lax.rsqrt(var + EPS) * g + b


def _post_body(att_ref, ssd_ref, x_ref, pe_ref, woa_ref, wos_ref, wg_ref, wu_ref, wfo_ref, wpe_ref, wpg_ref,
               g1_ref, b1_ref, g2_ref, b2_ref, o_ref, *, alpha, nff):
    mix = _dot(att_ref[...], woa_ref[...]) + _dot(ssd_ref[...], wos_ref[...])
    h = _layer_norm(alpha * x_ref[...] + mix, g1_ref[...], b1_ref[...])
    hb = h.astype(BF16)
    ffn = None
    step = D_FF // nff
    for c in range(nff):
        sl = slice(c * step, (c + 1) * step)
        gate = _dot(hb, wg_ref[:, sl])
        up = _dot(hb, wu_ref[:, sl])
        part = _dot((_silu(gate) * up).astype(BF16), wfo_ref[sl, :])
        ffn = part if ffn is None else ffn + part
    pemb = _dot(pe_ref[...].astype(BF16), wpe_ref[...]) * jax.nn.sigmoid(_dot(hb, wpg_ref[...]))
    o_ref[...] = _layer_norm(alpha * h + ffn + pemb, g2_ref[...], b2_ref[...])


def _post(att, ssd, x2d, pe2d, w, *, tm, alpha):
    T = x2d.shape[0]
    row = lambda width: pl.BlockSpec((tm, width), lambda i: (i, 0))
    full = lambda shape: pl.BlockSpec(shape, lambda i: (0, 0), pipeline_mode=pl.Buffered(1))
    return pl.pallas_call(
        functools.partial(_post_body, alpha=alpha, nff=2),
        grid=(T // tm,),
        in_specs=[row(1024), row(1024), row(D_MODEL), row(P_DIM),
                  full((1024, D_MODEL)), full((1024, D_MODEL)), full((D_MODEL, D_FF)), full((D_MODEL, D_FF)),
                  full((D_FF, D_MODEL)), full((P_DIM, D_MODEL)), full((D_MODEL, D_MODEL)),
                  full((1, D_MODEL)), full((1, D_MODEL)), full((1, D_MODEL)), full((1, D_MODEL))],
        out_specs=row(D_MODEL),
        out_shape=jax.ShapeDtypeStruct((T, D_MODEL), F32),
        compiler_params=_cparams(("parallel",)), name="post",
    )(att, ssd, x2d, pe2d, w["woa"], w["wos"], w["wg"], w["wu"], w["wfo"], w["wpe"], w["wpg"],
      w["g1"], w["b1"], w["g2"], w["b2"])


def _rope_tables(pos):
    half = ROT_DIM // 2
    inv = ROPE_THETA ** (-jnp.arange(half, dtype=F32) * 2.0 / ROT_DIM)
    ang = pos.astype(F32)[:, None] * inv[None, :]
    cos, sin = jnp.cos(ang), jnp.sin(ang)
    n = pos.shape[0]
    pad = jnp.zeros((n, ATT_HD - ROT_DIM), F32)
    ra = jnp.concatenate([cos, cos, jnp.ones((n, ATT_HD - ROT_DIM), F32)], 1)
    rb = jnp.concatenate([-sin, jnp.zeros((n, half), F32), pad], 1)
    rc = jnp.concatenate([jnp.zeros((n, half), F32), sin, pad], 1)
    rep = LANES // ATT_HD
    return tuple(jnp.tile(t, (1, rep)) for t in (ra, rb, rc))


def _layer(depth, lam_init, x_prompt, x_sample, p_prompt, p_sample, cache_k, cache_v, page_table, state_ssm,
           state_conv, prm):
    (w_in, conv_w, conv_b, dt_bias, a_log, d_skip, ssd_norm_w, lq1, lk1, lq2, lk2, subln_w,
     w_out, ln1_g, ln1_b, w_ffn_in, w_ffn_out, w_pe, w_pg, ln2_g, ln2_b) = prm
    bp, lp, _ = x_prompt.shape
    bs, ls, _ = x_sample.shape
    npages = page_table.shape[1]
    page = cache_k.shape[1]
    past_len = npages * page
    alpha = (2 * depth) ** 0.25

    wb = w_in.astype(BF16)
    wq = wb[:, 0:1024]
    wk = wb[:, 1024:2048]
    wv = wb[:, 2048:3072]
    wz = wb[:, 3072:4096]
    wxbc = wb[:, 4096:4096 + CONV_DIM]
    wdt = jnp.pad(wb[:, 4096 + CONV_DIM:], ((0, 0), (0, LANES - SSD_HEADS)))
    w1 = dict(wq=wq, wk=wk, wv=wv, wz=wz, wxbc=wxbc, wdt=wdt)

    hrep = jnp.arange(SSD_HEADS * SSD_HD) // SSD_HD
    e64 = (jnp.arange(LANES)[:, None] == hrep[None, :]).astype(BF16)
    e128 = (jnp.arange(LANES)[:, None] == (jnp.arange(SSD_HEADS * LANES) // LANES)[None, :]).astype(BF16)
    padl = lambda v: jnp.pad(v.astype(F32), (0, LANES - v.shape[0]))[None, :]
    pssd = dict(e64=jnp.tile(e64, (3, 1)), e128=jnp.tile(e128, (3, 1)), conv_w=conv_w.astype(F32), conv_b=conv_b.astype(F32)[None, :],
                dt_bias=padl(dt_bias), a_log=padl(a_log), d_skip=jnp.repeat(d_skip.astype(F32), SSD_HD)[None, :],
                norm_w=ssd_norm_w.astype(F32)[None, :])
    lams = [v.astype(F32)[None, :] for v in (lq1, lk1, lq2, lk2)]
    subln = subln_w.astype(F32)[None, :]
    wob = w_out.astype(BF16)
    wfi = w_ffn_in.astype(BF16)
    w4 = dict(woa=wob[:ATT_WIDTH], wos=wob[ATT_WIDTH:], wg=wfi[:, :D_FF], wu=wfi[:, D_FF:],
              wfo=w_ffn_out.astype(BF16), wpe=w_pe.astype(BF16), wpg=w_pg.astype(BF16),
              g1=ln1_g.astype(F32)[None, :], b1=ln1_b.astype(F32)[None, :],
              g2=ln2_g.astype(F32)[None, :], b2=ln2_b.astype(F32)[None, :])

    tm = 256
    xp2 = x_prompt.reshape(bp * lp, D_MODEL)
    q, kt, vf, z, xbc, dt, kb, vt = _inproj(xp2, w1, _rope_tables(jnp.arange(lp)), tm=tm, nbatch=bp,
                                            attn_copies=True)
    k_p = jnp.transpose(kt.reshape(bp, 2 * ATT_HEADS, ATT_HD, lp), (0, 3, 1, 2))
    ssd_p, ssm_p = _ssd_prompt(xbc, dt, z, pssd, bp)
    att_p = _flash(q, kb, vt, lams, subln, lam_init=lam_init)
    y_p = _post(att_p, ssd_p, xp2, p_prompt.reshape(bp * lp, P_DIM), w4, tm=2 * tm, alpha=alpha)
    conv_p = xbc.reshape(bp, lp, CONV_DIM)[:, lp - (CONV_W - 1):, :]

    hist = CONV_W - 1
    x8 = jnp.pad(x_sample.astype(F32), ((0, 0), (hist, SROWS - hist - ls), (0, 0)))
    pos8 = jnp.tile(jnp.clip(jnp.arange(SROWS) - hist, 0, ls - 1) + past_len, bs)
    qs, kfs, vfs, zs, xbcs, dts = _inproj(x8.reshape(bs * SROWS, D_MODEL), w1, _rope_tables(pos8),
                                          tm=256, nbatch=1, attn_copies=False)
    xbc8 = jnp.concatenate([state_conv.astype(F32), xbcs.reshape(bs, SROWS, CONV_DIM)[:, hist:]], axis=1)
    ssd_s8, ssm_s = _ssd_sample(xbc8.reshape(bs * SROWS, CONV_DIM), dts, zs,
                                state_ssm.reshape(bs, SSD_HEADS * SSD_HD, D_STATE), pssd)
    tok = lambda a, w: a.reshape(bs, SROWS, w)[:, hist:hist + ls]
    ncomp = 2 * ATT_HEADS
    q4 = tok(qs, 1024).reshape(bs, ls, ncomp, ATT_HD)
    qcq = jnp.transpose(q4, (0, 2, 1, 3)).reshape(bs, ncomp * ls, ATT_HD)
    slot = (jnp.arange(ncomp * ls)[:, None] // ls) == (jnp.arange(ncomp * ATT_HD)[None, :] // ATT_HD)
    qbd = jnp.where(slot[None], jnp.tile(qcq, (1, 1, ncomp)), jnp.zeros((), BF16))
    k_s = tok(kfs, 1024)
    v_s = tok(vfs, 1024)
    vexp = jnp.repeat(v_s.reshape(bs, ls, ATT_HEADS, 2 * ATT_HD), 2 * ls, axis=2)
    att_s = _decode_attn(qbd, kfs.reshape(bs, SROWS, 1024), vexp,
                         jnp.transpose(cache_k, (0, 2, 3, 1)),
                         cache_v.reshape(cache_v.shape[0], page * ATT_HEADS, 2 * ATT_HD),
                         page_table, lams, subln, lam_init=lam_init)
    att_s = att_s.reshape(bs, ATT_HEADS, 2, ls, LANES)[:, :, 0]
    att_s = jnp.transpose(att_s, (0, 2, 1, 3)).reshape(bs * ls, ATT_WIDTH)
    ssd_s = tok(ssd_s8, SSD_INNER).reshape(bs * ls, SSD_INNER)
    y_s = _post(att_s, ssd_s, x_sample.reshape(bs * ls, D_MODEL), p_sample.reshape(bs * ls, P_DIM), w4,
                tm=min(256, bs * ls), alpha=alpha)
    conv_s = tok(xbcs, CONV_DIM)[:, ls - hist:]

    return (y_p.reshape(bp, lp, D_MODEL), y_s.reshape(bs, ls, D_MODEL),
            k_p, vf.reshape(bp, lp, ATT_HEADS, 2 * ATT_HD),
            ssm_p.reshape(bp, SSD_HEADS, SSD_HD, D_STATE), conv_p,
            k_s.reshape(bs, ls, 2 * ATT_HEADS, ATT_HD), v_s.reshape(bs, ls, ATT_HEADS, 2 * ATT_HD),
            ssm_s.reshape(bs, SSD_HEADS, SSD_HD, D_STATE), conv_s)


def kernel(x_prompt, x_sample, p_prompt, p_sample, cache_k, cache_v, page_table, state_ssm, state_conv, w_in, conv_w, conv_b, dt_bias, a_log, d_skip, ssd_norm_w, lambda_q1, lambda_k1, lambda_q2, lambda_k2, subln_w, w_out, ln1_g, ln1_b, w_ffn_in, w_ffn_out, w_pe, w_pg, ln2_g, ln2_b):
    depth = w_in.shape[0]
    assert depth == 1, "single-layer trunk"
    prm = tuple(a[0] for a in (w_in, conv_w, conv_b, dt_bias, a_log, d_skip, ssd_norm_w, lambda_q1, lambda_k1,
                               lambda_q2, lambda_k2, subln_w, w_out, ln1_g, ln1_b, w_ffn_in, w_ffn_out, w_pe,
                               w_pg, ln2_g, ln2_b))
    lam_init = 0.8 - 0.6 * math.exp(-0.3 * 0)
    outs = _layer(depth, lam_init, x_prompt, x_sample, p_prompt[0], p_sample[0], cache_k[0], cache_v[0], page_table,
                  state_ssm[0], state_conv[0], prm)
    y_p, y_s = outs[0], outs[1]
    return (y_p, y_s) + tuple(o[None] for o in outs[2:])
```
